```python
import functools
import jax, jax.numpy as jnp
from jax import lax
import numpy as np


D_MODEL = 1024
BATCH = 8
SEQ = 4096
DEPTH = 4

GRID_W = 64
CTX_LEN = 256
N_MIXERS = 4
NORM_EPS = 1e-6
FNET_GROUPS = 4
ATTN_HEAD_DIM = 64
ATTN_Q_HEADS = D_MODEL // ATTN_HEAD_DIM
ATTN_KV_HEADS = 2
ATTN_GROUP = ATTN_Q_HEADS // ATTN_KV_HEADS
WINDOW = 128
ATTN_BLOCK = 128
ROPE_THETA = 10000.0
ROPE_PAIRS = ATTN_HEAD_DIM // 4
CONV_WIDTH = 31
GMLP_CHUNK = 128
GMLP_GROUPS = 4
GMLP_HALF = D_MODEL
D_FF = 4 * D_MODEL

kernel_name = 'hybrid_fourier_swa_conv_gmlp_prefix_dit'


def _rms_norm(x, g):
    xf = x.astype(jnp.float32)
    y = xf * lax.rsqrt(jnp.mean(jnp.square(xf), axis=-1, keepdims=True) + NORM_EPS)
    return (y * g.astype(jnp.float32)).astype(x.dtype)


def _layer_norm(x, g, b):
    xf = x.astype(jnp.float32)
    mu = jnp.mean(xf, axis=-1, keepdims=True)
    var = jnp.mean(jnp.square(xf - mu), axis=-1, keepdims=True)
    y = (xf - mu) * lax.rsqrt(var + NORM_EPS)
    return (y * g.astype(jnp.float32) + b.astype(jnp.float32)).astype(x.dtype)


def _modulate(h, shift, scale):
    return h * (1.0 + scale) + shift


def _axial_rope_tables(rows):
    row = jnp.repeat(jnp.arange(rows, dtype=jnp.float32), GRID_W)
    col = jnp.tile(jnp.arange(GRID_W, dtype=jnp.float32), rows)
    inv = ROPE_THETA ** (-jnp.arange(ROPE_PAIRS, dtype=jnp.float32) / ROPE_PAIRS)
    ang_r = row[:, None] * inv[None, :]
    ang_c = col[:, None] * inv[None, :]
    return (jnp.cos(ang_r), jnp.sin(ang_r), jnp.cos(ang_c), jnp.sin(ang_c))


def _rotate(x, cos, sin):
    x1, x2 = x[..., :ROPE_PAIRS], x[..., ROPE_PAIRS:]
    cos = cos[None, :, None, :]
    sin = sin[None, :, None, :]
    return jnp.concatenate([x1 * cos - x2 * sin, x2 * cos + x1 * sin], axis=-1)


def _apply_axial_rope(x, tabs):
    cr, sr, cc, sc = tabs
    xf = x.astype(jnp.float32)
    half = ATTN_HEAD_DIM // 2
    out = jnp.concatenate([_rotate(xf[..., :half], cr, sr), _rotate(xf[..., half:], cc, sc)], axis=-1)
    return out.astype(x.dtype)


def _fourier_mixer(h, w_out, b_out):
    b, s, _ = h.shape
    hf = h.astype(jnp.float32).reshape(b, s, FNET_GROUPS, D_MODEL // FNET_GROUPS)
    mixed = jnp.fft.fft2(hf, axes=(1, 3), norm='ortho').real
    mixed = mixed.reshape(b, s, D_MODEL).astype(h.dtype)
    return mixed @ w_out + b_out


def _attn_qkv(h, w_qkv, q_g, k_g):
    b, s, _ = h.shape
    nq = ATTN_Q_HEADS * ATTN_HEAD_DIM
    nk = ATTN_KV_HEADS * ATTN_HEAD_DIM
    qkv = h @ w_qkv
    q = _rms_norm(qkv[..., :nq].reshape(b, s, ATTN_Q_HEADS, ATTN_HEAD_DIM), q_g)
    k = _rms_norm(qkv[..., nq:nq + nk].reshape(b, s, ATTN_KV_HEADS, ATTN_HEAD_DIM), k_g)
    v = qkv[..., nq + nk:].reshape(b, s, ATTN_KV_HEADS, ATTN_HEAD_DIM)
    return q, k, v


def _sink_softmax(scores, sink_kg):
    s_sink = jnp.broadcast_to(sink_kg[:, :, None, None], scores.shape[:-1] + (1,))
    p = jax.nn.softmax(jnp.concatenate([scores, s_sink], axis=-1), axis=-1)
    return p[..., :-1]


def _banded_attention(q, k, v, k_ctx, v_ctx, sink_kg):
    b, seq_len = q.shape[0], q.shape[1]
    nb = seq_len // ATTN_BLOCK
    scale = ATTN_HEAD_DIM ** -0.5
    n_band = 3 * ATTN_BLOCK
    qb = q.reshape(b, nb, ATTN_BLOCK, ATTN_KV_HEADS, ATTN_GROUP, ATTN_HEAD_DIM).transpose(1, 0, 2, 3, 4, 5)

    def band(t):
        tb = t.reshape(b, nb, ATTN_BLOCK, ATTN_KV_HEADS, ATTN_HEAD_DIM)
        tp = jnp.pad(tb, ((0, 0), (1, 1), (0, 0), (0, 0), (0, 0)))
        return jnp.concatenate([tp[:, :-2], tp[:, 1:-1], tp[:, 2:]], axis=2).transpose(1, 0, 2, 3, 4)

    kb, vb = band(k), band(v)

    def one_block(args):
        qj, kj, vj, j = args
        qpos = j * ATTN_BLOCK + jnp.arange(ATTN_BLOCK)
        kpos = (j - 1) * ATTN_BLOCK + jnp.arange(n_band)
        valid = (jnp.abs(qpos[:, None] - kpos[None, :]) <= WINDOW) & (kpos >= 0)[None, :] & (kpos < seq_len)[None, :]
        s_win = jnp.einsum('bqkgd,bskd->bkgqs', qj, kj).astype(jnp.float32) * scale
        s_win = jnp.where(valid, s_win, -jnp.inf)
        s_ctx = jnp.einsum('bqkgd,blkd->bkgql', qj, k_ctx).astype(jnp.float32) * scale
        p = _sink_softmax(jnp.concatenate([s_win, s_ctx], axis=-1), sink_kg)
        p_win = p[..., :n_band].astype(vj.dtype)
        p_ctx = p[..., n_band:].astype(v_ctx.dtype)
        return jnp.einsum('bkgqs,bskd->bqkgd', p_win, vj) + jnp.einsum('bkgql,blkd->bqkgd', p_ctx, v_ctx)

    out = lax.map(one_block, (qb, kb, vb, jnp.arange(nb)))
    return out.transpose(1, 0, 2, 3, 4, 5).reshape(b, seq_len, ATTN_Q_HEADS * ATTN_HEAD_DIM)


def _context_attention(qc, kc, vc, sink_kg):
    b, l = qc.shape[0], qc.shape[1]
    scale = ATTN_HEAD_DIM ** -0.5
    qg = qc.reshape(b, l, ATTN_KV_HEADS, ATTN_GROUP, ATTN_HEAD_DIM)
    sc = jnp.einsum('blkgd,bmkd->bkglm', qg, kc).astype(jnp.float32) * scale
    p = _sink_softmax(sc, sink_kg).astype(vc.dtype)
    return jnp.einsum('bkglm,bmkd->blkgd', p, vc).reshape(b, l, ATTN_Q_HEADS * ATTN_HEAD_DIM)


def _window_gqa_mixer(h_lat, h_ctx, rope, w_qkv, q_g, k_g, sink, w_o, with_ctx_out):
    q, k, v = _attn_qkv(h_lat, w_qkv, q_g, k_g)
    q = _apply_axial_rope(q, rope)
    k = _apply_axial_rope(k, rope)
    qc, kc, vc = _attn_qkv(h_ctx, w_qkv, q_g, k_g)
    sink_kg = sink.astype(jnp.float32).reshape(ATTN_KV_HEADS, ATTN_GROUP)
    y_lat = _banded_attention(q, k, v, kc, vc, sink_kg) @ w_o
    y_ctx = _context_attention(qc, kc, vc, sink_kg) @ w_o if with_ctx_out else None
    return y_lat, y_ctx


def _conv_module(h, w_in, b_in, dw, dw_b, ln_g, ln_b, w_out, b_out):
    a = h @ w_in + b_in
    u = a[..., :D_MODEL] * jax.nn.sigmoid(a[..., D_MODEL:])
    u = lax.conv_general_dilated(u, dw[:, None, :].astype(u.dtype), window_strides=(1,),
                                 padding=[(CONV_WIDTH // 2, CONV_WIDTH // 2)],
                                 dimension_numbers=('NWC', 'WIO', 'NWC'),
                                 feature_group_count=D_MODEL) + dw_b
    u = jax.nn.silu(_layer_norm(u, ln_g, ln_b))
    return u @ w_out + b_out


def _chunk_gmlp(h, w_in, b_in, ln_g, ln_b, w_s, b_s, w_out):
    b, s, _ = h.shape
    z = jax.nn.gelu(h @ w_in + b_in)
    u, v = z[..., :GMLP_HALF], z[..., GMLP_HALF:]
    v = _layer_norm(v, ln_g, ln_b)
    nc = s // GMLP_CHUNK
    vc = v.reshape(b, nc, GMLP_CHUNK, GMLP_GROUPS, GMLP_HALF // GMLP_GROUPS)
    sg = jnp.einsum('gpq,bnqgd->bnpgd', w_s, vc) + b_s.T[None, None, :, :, None]
    return (u * sg.reshape(b, s, GMLP_HALF)) @ w_out


def _sq_relu_mlp(h, w1, w2):
    return jnp.square(jax.nn.relu(h @ w1)) @ w2


def setup_inputs(seed: int = 0) -> dict:
    key = jax.random.key(seed)
    ks = iter(jax.random.split(key, 40))

    def nrm(shape, s):
        return jax.random.normal(next(ks), shape, jnp.float32) * s

    n_a, n_b, n_c, n_d = (len(range(m, DEPTH, N_MIXERS)) for m in range(N_MIXERS))
    D = D_MODEL
    qkv_w = (ATTN_Q_HEADS + 2 * ATTN_KV_HEADS) * ATTN_HEAD_DIM
    gd = GMLP_HALF // GMLP_GROUPS
    return {
        'x': nrm((BATCH, SEQ, D), 1.0),
        'c': nrm((BATCH, D), 1.0),
        'ctx': nrm((BATCH, CTX_LEN, D), 1.0),
        'c_ctx': nrm((D,), 1.0),
        'ada_w': nrm((DEPTH, D, 6 * D), 0.5 * D ** -0.5),
        'ada_b': nrm((DEPTH, 6 * D), 0.02),
        'norm1_g': 1.0 + nrm((DEPTH, D), 0.05),
        'norm2_g': 1.0 + nrm((DEPTH, D), 0.05),
        'mlp_w1': nrm((DEPTH, D, D_FF), D ** -0.5),
        'mlp_w2': nrm((DEPTH, D_FF, D), D_FF ** -0.5),
        'fnet_w': nrm((n_a, D, D), D ** -0.5),
        'fnet_b': nrm((n_a, D), 0.02),
        'attn_w_qkv': nrm((n_b, D, qkv_w), D ** -0.5),
        'attn_q_g': 1.0 + nrm((n_b, ATTN_HEAD_DIM), 0.05),
        'attn_k_g': 1.0 + nrm((n_b, ATTN_HEAD_DIM), 0.05),
        'attn_sink': nrm((n_b, ATTN_Q_HEADS), 0.5),
        'attn_w_o': nrm((n_b, D, D), D ** -0.5),
        'conv_w_in': nrm((n_c, D, 2 * D), D ** -0.5),
        'conv_b_in': nrm((n_c, 2 * D), 0.02),
        'conv_dw': nrm((n_c, CONV_WIDTH, D), CONV_WIDTH ** -0.5),
        'conv_dw_b': nrm((n_c, D), 0.02),
        'conv_ln_g': 1.0 + nrm((n_c, D), 0.05),
        'conv_ln_b': nrm((n_c, D), 0.02),
        'conv_w_out': nrm((n_c, D, D), D ** -0.5),
        'conv_b_out': nrm((n_c, D), 0.02),
        'gmlp_w_in': nrm((n_d, D, 2 * GMLP_HALF), D ** -0.5),
        'gmlp_b_in': nrm((n_d, 2 * GMLP_HALF), 0.02),
        'gmlp_ln_g': 1.0 + nrm((n_d, GMLP_HALF), 0.05),
        'gmlp_ln_b': nrm((n_d, GMLP_HALF), 0.02),
        'gmlp_w_s': nrm((n_d, GMLP_GROUPS, GMLP_CHUNK, GMLP_CHUNK), 0.5 * GMLP_CHUNK ** -0.5),
        'gmlp_b_s': 1.0 + nrm((n_d, GMLP_GROUPS, GMLP_CHUNK), 0.1),
        'gmlp_w_out': nrm((n_d, GMLP_HALF, D), GMLP_HALF ** -0.5),
    }


def reference(x, c, ctx, c_ctx, ada_w, ada_b, norm1_g, norm2_g, mlp_w1, mlp_w2,
              fnet_w, fnet_b, attn_w_qkv, attn_q_g, attn_k_g, attn_sink, attn_w_o,
              conv_w_in, conv_b_in, conv_dw, conv_dw_b, conv_ln_g, conv_ln_b, conv_w_out, conv_b_out,
              gmlp_w_in, gmlp_b_in, gmlp_ln_g, gmlp_ln_b, gmlp_w_s, gmlp_b_s, gmlp_w_out):
    ROWS = x.shape[1] // GRID_W
    rope = _axial_rope_tables(ROWS)
    silu_c = jax.nn.silu(c)
    silu_cc = jax.nn.silu(c_ctx)
    h_lat, h_ctx = x, ctx
    for i in range(DEPTH):
        last = i == DEPTH - 1
        kind, li = i % N_MIXERS, i // N_MIXERS
        sh1, sc1, g1, sh2, sc2, g2 = jnp.split((silu_c @ ada_w[i] + ada_b[i])[:, None, :], 6, axis=-1)
        csh1, csc1, cg1, csh2, csc2, cg2 = jnp.split(silu_cc @ ada_w[i] + ada_b[i], 6, axis=-1)
        need_ctx_in = (not last) or kind == 1
        n_lat = _modulate(_rms_norm(h_lat, norm1_g[i]), sh1, sc1)
        n_ctx = _modulate(_rms_norm(h_ctx, norm1_g[i]), csh1, csc1) if need_ctx_in else None
        if kind == 1:
            y_lat, y_ctx = _window_gqa_mixer(n_lat, n_ctx, rope, attn_w_qkv[li], attn_q_g[li], attn_k_g[li],
                                             attn_sink[li], attn_w_o[li], not last)
        else:
            if kind == 0:
                mix = functools.partial(_fourier_mixer, w_out=fnet_w[li], b_out=fnet_b[li])
            elif kind == 2:
                mix = functools.partial(_conv_module, w_in=conv_w_in[li], b_in=conv_b_in[li], dw=conv_dw[li],
                                        dw_b=conv_dw_b[li], ln_g=conv_ln_g[li], ln_b=conv_ln_b[li],
                                        w_out=conv_w_out[li], b_out=conv_b_out[li])
            else:
                mix = functools.partial(_chunk_gmlp, w_in=gmlp_w_in[li], b_in=gmlp_b_in[li], ln_g=gmlp_ln_g[li],
                                        ln_b=gmlp_ln_b[li], w_s=gmlp_w_s[li], b_s=gmlp_b_s[li],
                                        w_out=gmlp_w_out[li])
            y_lat = mix(n_lat)
            y_ctx = mix(n_ctx) if not last else None
        h_lat = h_lat + g1 * y_lat
        h_lat = h_lat + g2 * _sq_relu_mlp(_modulate(_rms_norm(h_lat, norm2_g[i]), sh2, sc2), mlp_w1[i], mlp_w2[i])
        if not last:
            h_ctx = h_ctx + cg1 * y_ctx
            h_ctx = h_ctx + cg2 * _sq_relu_mlp(_modulate(_rms_norm(h_ctx, norm2_g[i]), csh2, csc2),
                                               mlp_w1[i], mlp_w2[i])
    return h_lat
```

```python
import functools
import math

import numpy as np
import jax
import jax.numpy as jnp
from jax import lax
from jax.experimental import pallas as pl
from jax.experimental.pallas import tpu as pltpu

F32 = jnp.float32
BF16 = jnp.bfloat16

NORM_EPS = 1e-6
GRID_W = 64
FNET_GROUPS = 4
HEAD_DIM = 64
KV_HEADS = 2
WINDOW = 128
ATTN_BLOCK = 128
ROPE_THETA = 10000.0
ROPE_PAIRS = HEAD_DIM // 4
CONV_WIDTH = 31
CONV_HALO = 16
GMLP_CHUNK = 128
GMLP_GROUPS = 4
N_MIXERS = 4
LANES = 128
SUBLANES = 8
VMEM_LIMIT = 56 * 1024 * 1024
NEG_BIG = -1e30


def _resident(shape):
    nd = len(shape)
    return pl.BlockSpec(shape, lambda *_: (0,) * nd, pipeline_mode=pl.Buffered(1))


def _params(n_grid):
    return pltpu.CompilerParams(dimension_semantics=("parallel",) * n_grid, vmem_limit_bytes=VMEM_LIMIT)


def _row_tile(s):
    return 512 if s % 512 == 0 else s


def _dot(a, b):
    return jnp.dot(a, b, preferred_element_type=F32)


def _dot_nt(a, b):
    return lax.dot_general(a, b, (((1,), (1,)), ((), ())), preferred_element_type=F32)


def _norm_mod(x, g, mod_ref, row):
    ms = jnp.mean(x * x, axis=-1, keepdims=True)
    y = x * lax.rsqrt(ms + NORM_EPS) * g
    return y * (1.0 + mod_ref[0, row + 1:row + 2, :]) + mod_ref[0, row:row + 1, :]


def _layer_norm(x, g, b):
    mu = jnp.mean(x, axis=-1, keepdims=True)
    xc = x - mu
    var = jnp.mean(xc * xc, axis=-1, keepdims=True)
    return xc * lax.rsqrt(var + NORM_EPS) * g + b


def _adaln_kernel(c_ref, w_ref, b_ref, o_ref):
    c = c_ref[...]
    s = (c * jax.nn.sigmoid(c)).astype(BF16)
    o_ref[0] = _dot(s, w_ref[0].astype(BF16)) + b_ref[0]


def _adaln(c_rows, ada_w, ada_b):
    depth, d, n = ada_w.shape
    r = c_rows.shape[0]
    tn = 1024
    return pl.pallas_call(
        _adaln_kernel,
        out_shape=jax.ShapeDtypeStruct((depth, r, n), F32),
        grid=(depth, n // tn),
        in_specs=[pl.BlockSpec((r, d), lambda l, j: (0, 0)),
                  pl.BlockSpec((1, d, tn), lambda l, j: (l, 0, j)),
                  pl.BlockSpec((1, 1, tn), lambda l, j: (l, 0, j))],
        out_specs=pl.BlockSpec((1, r, tn), lambda l, j: (l, 0, j)),
        compiler_params=_params(2),
        name="adaln",
    )(c_rows, ada_w, ada_b.reshape(depth, 1, n))


def _mlp_kernel(h_ref, mod_ref, g_ref, w1_ref, w2_ref, o_ref, *, ff_chunk):
    x = h_ref[0]
    n = _norm_mod(x, g_ref[...], mod_ref, 3).astype(BF16)
    d_ff = w1_ref.shape[1]
    acc = jnp.zeros(x.shape, F32)
    for c in range(d_ff // ff_chunk):
        a = _dot(n, w1_ref[:, c * ff_chunk:(c + 1) * ff_chunk])
        a = jnp.square(jnp.maximum(a, 0.0)).astype(BF16)
        acc = acc + _dot(a, w2_ref[c * ff_chunk:(c + 1) * ff_chunk, :])
    o_ref[0] = x + mod_ref[0, 5:6, :] * acc


def _mlp(h, mod, g, w1, w2):
    b, s, d = h.shape
    t = _row_tile(s)
    d_ff = w1.shape[1]
    return pl.pallas_call(
        functools.partial(_mlp_kernel, ff_chunk=1024),
        out_shape=jax.ShapeDtypeStruct(h.shape, F32),
        grid=(b, s // t),
        in_specs=[pl.BlockSpec((1, t, d), lambda i, j: (i, j, 0)),
                  pl.BlockSpec((1, 6, d), lambda i, j: (i, 0, 0)),
                  _resident((1, d)), _resident((d, d_ff)), _resident((d_ff, d))],
        out_specs=pl.BlockSpec((1, t, d), lambda i, j: (i, j, 0)),
        compiler_params=_params(2),
        name="mlp",
    )(h, mod, g.reshape(1, d), w1, w2)


def _proj_res_kernel(y_ref, w_ref, b_ref, h_ref, mod_ref, o_ref):
    acc = _dot(y_ref[0], w_ref[...]) + b_ref[...]
    o_ref[0] = h_ref[0] + mod_ref[0, 2:3, :] * acc


def _proj_res(y, w, bias, h, mod):
    b, s, d = h.shape
    k = y.shape[-1]
    t = _row_tile(s)
    return pl.pallas_call(
        _proj_res_kernel,
        out_shape=jax.ShapeDtypeStruct(h.shape, F32),
        grid=(b, s // t),
        in_specs=[pl.BlockSpec((1, t, k), lambda i, j: (i, j, 0)),
                  _resident((k, d)), _resident((1, d)),
                  pl.BlockSpec((1, t, d), lambda i, j: (i, j, 0)),
                  pl.BlockSpec((1, 6, d), lambda i, j: (i, 0, 0))],
        out_specs=pl.BlockSpec((1, t, d), lambda i, j: (i, j, 0)),
        compiler_params=_params(2),
        name="proj_res",
    )(y, w, bias.reshape(1, d), h, mod)


def _fnet_chan_kernel(h_ref, mod_ref, g_ref, cs_ref, p_ref, q_ref, *, gc):
    n = _norm_mod(h_ref[0], g_ref[...], mod_ref, 0).astype(BF16)
    for gi in range(n.shape[1] // gc):
        r = _dot(n[:, gi * gc:(gi + 1) * gc], cs_ref[...])
        p_ref[0, :, gi * gc:(gi + 1) * gc] = r[:, :gc].astype(BF16)
        q_ref[0, :, gi * gc:(gi + 1) * gc] = r[:, gc:].astype(BF16)


def _fnet_pos_kernel(p_ref, q_ref, tw_ref, csm_ref, o_ref, y_ref, *, radix, m, rc):
    p_idx = pl.program_id(2)
    tc = p_ref.shape[2]

    @pl.when(p_idx == 0)
    def _():
        if radix == 1:
            y_ref[0, 0:m, :] = p_ref[0]
            y_ref[0, m:2 * m, :] = -q_ref[0]
            return

        def chunk(i, carry):
            r0 = pl.multiple_of(i * rc, rc)
            a = [p_ref[0, pl.ds(q * m + r0, rc), :].astype(F32) for q in range(4)]
            b = [-q_ref[0, pl.ds(q * m + r0, rc), :].astype(F32) for q in range(4)]
            t0r, t0i = a[0] + a[2], b[0] + b[2]
            t1r, t1i = a[0] - a[2], b[0] - b[2]
            t2r, t2i = a[1] + a[3], b[1] + b[3]
            t3r, t3i = a[1] - a[3], b[1] - b[3]
            bs = [(t0r + t2r, t0i + t2i), (t1r + t3i, t1i - t3r),
                  (t0r - t2r, t0i - t2i), (t1r - t3i, t1i + t3r)]
            for pp, (br, bi) in enumerate(bs):
                if pp == 0:
                    yr, yi = br, bi
                else:
                    c = jnp.tile(tw_ref[pp - 1, 0, pl.ds(r0, rc), :], (1, tc // LANES))
                    s = jnp.tile(tw_ref[pp - 1, 1, pl.ds(r0, rc), :], (1, tc // LANES))
                    yr = c * br + s * bi
                    yi = c * bi - s * br
                y_ref[pp, pl.ds(r0, rc), :] = yr.astype(BF16)
                y_ref[pp, pl.ds(m + r0, rc), :] = yi.astype(BF16)
            return carry

        lax.fori_loop(0, m // rc, chunk, 0)

    o_ref[0] = _dot(csm_ref[...], y_ref[p_idx]).astype(BF16)


def _fnet_tables(s, gc):
    radix = 4 if s % 512 == 0 else 1
    m = s // radix
    ang_c = 2.0 * np.pi * np.outer(np.arange(gc), np.arange(gc)) / gc
    cs = np.concatenate([np.cos(ang_c), np.sin(ang_c)], axis=1)
    ang_m = 2.0 * np.pi * np.outer(np.arange(m), np.arange(m)) / m
    csm = np.concatenate([np.cos(ang_m), np.sin(ang_m)], axis=1) / math.sqrt(s * gc)
    if radix == 4:
        ang_t = 2.0 * np.pi * np.outer(np.arange(1, 4), np.arange(m)) / s
        tw = np.stack([np.cos(ang_t), np.sin(ang_t)], axis=1)
        tw = np.broadcast_to(tw[..., None], (3, 2, m, LANES))
    else:
        tw = np.zeros((1, 2, 8, LANES))
    return (radix, m, jnp.asarray(cs, BF16), jnp.asarray(csm, BF16), jnp.asarray(tw, F32))


def _fourier_mixer(h, mod, g, w_out, b_out):
    b, s, d = h.shape
    gc = d // FNET_GROUPS
    t = _row_tile(s)
    radix, m, cs, csm, tw = _fnet_tables(s, gc)
    p, q = pl.pallas_call(
        functools.partial(_fnet_chan_kernel, gc=gc),
        out_shape=(jax.ShapeDtypeStruct((b, s, d), BF16),) * 2,
        grid=(b, s // t),
        in_specs=[pl.BlockSpec((1, t, d), lambda i, j: (i, j, 0)),
                  pl.BlockSpec((1, 6, d), lambda i, j: (i, 0, 0)),
                  _resident((1, d)), _resident((gc, 2 * gc))],
        out_specs=(pl.BlockSpec((1, t, d), lambda i, j: (i, j, 0)),) * 2,
        compiler_params=_params(2),
        name="fnet_chan",
    )(h, mod, g.reshape(1, d), cs)
    tc = 256
    nct = d // tc
    mixed = pl.pallas_call(
        functools.partial(_fnet_pos_kernel, radix=radix, m=m, rc=min(m, 128)),
        out_shape=jax.ShapeDtypeStruct((b, m, radix * d), BF16),
        grid=(b, nct, radix),
        in_specs=[pl.BlockSpec((1, s, tc), lambda i, j, k: (i, 0, j)),
                  pl.BlockSpec((1, s, tc), lambda i, j, k: (i, 0, j)),
                  _resident(tw.shape), _resident((m, 2 * m))],
        out_specs=pl.BlockSpec((1, m, tc), lambda i, j, k: (i, 0, k * nct + j)),
        scratch_shapes=[pltpu.VMEM((radix, 2 * m, tc), BF16)],
        compiler_params=pltpu.CompilerParams(dimension_semantics=("parallel", "parallel", "arbitrary"),
                                             vmem_limit_bytes=VMEM_LIMIT),
        name="fnet_pos",
    )(p, q, tw, csm)
    mixed = mixed.reshape(b, s, d)
    return _proj_res(mixed, w_out.astype(BF16), b_out, h, mod)


def _qkv_kernel(h_ref, mod_ref, g_ref, w_ref, ind_ref, exp_ref, hg_ref, cos_ref, sin_ref,
                q_ref, k_ref, v_ref, *, nq, nk):
    n = _norm_mod(h_ref[0], g_ref[...], mod_ref, 0).astype(BF16)
    acc = _dot(n, w_ref[...])
    qk = acc[:, :nq + nk]
    ssq = _dot((qk * qk).astype(BF16), ind_ref[...])
    r = lax.rsqrt(ssq * (1.0 / HEAD_DIM) + NORM_EPS)
    r_hi = r.astype(BF16)
    r_lo = (r - r_hi.astype(F32)).astype(BF16)
    rexp = _dot(r_hi, exp_ref[...]) + _dot(r_lo, exp_ref[...])
    qk = qk * rexp * hg_ref[...]
    lane = lax.broadcasted_iota(jnp.int32, (1, LANES), 1)
    first_half = (lane % (2 * ROPE_PAIRS)) < ROPE_PAIRS
    cos = cos_ref[...]
    sin = sin_ref[...]
    for c in range((nq + nk) // LANES):
        x = qk[:, c * LANES:(c + 1) * LANES]
        partner = jnp.where(first_half, pltpu.roll(x, LANES - ROPE_PAIRS, 1), pltpu.roll(x, ROPE_PAIRS, 1))
        y = (x * cos + partner * sin).astype(BF16)
        if c < nq // LANES:
            q_ref[0, :, c * LANES:(c + 1) * LANES] = y
        else:
            k_ref[0, :, c * LANES - nq:(c + 1) * LANES - nq] = y
    v_ref[0] = acc[:, nq + nk:].astype(BF16)


def _rope_tables(s, with_rope):
    if not with_rope:
        return jnp.ones((s, LANES), F32), jnp.zeros((s, LANES), F32)
    rows = s // GRID_W
    row = jnp.repeat(jnp.arange(rows, dtype=F32), GRID_W)
    col = jnp.tile(jnp.arange(GRID_W, dtype=F32), rows)
    inv = ROPE_THETA ** (-jnp.arange(ROPE_PAIRS, dtype=F32) / ROPE_PAIRS)
    ang_r = row[:, None] * inv[None, :]
    ang_c = col[:, None] * inv[None, :]
    cos = jnp.concatenate([jnp.cos(ang_r)] * 2 + [jnp.cos(ang_c)] * 2, axis=1)
    sin = jnp.concatenate([-jnp.sin(ang_r), jnp.sin(ang_r), -jnp.sin(ang_c), jnp.sin(ang_c)], axis=1)
    return jnp.tile(cos, (1, LANES // HEAD_DIM)), jnp.tile(sin, (1, LANES // HEAD_DIM))


def _qkv(h, mod, g, w_dup, q_g, k_g, with_rope):
    b, s, d = h.shape
    nq = d
    nk = KV_HEADS * LANES
    t = _row_tile(s)
    n_heads = (nq + nk) // HEAD_DIM
    ind = np.zeros((nq + nk, LANES), np.float32)
    ind[np.arange(nq + nk), np.arange(nq + nk) // HEAD_DIM] = 1.0
    hg = jnp.concatenate([jnp.tile(q_g, nq // HEAD_DIM), jnp.tile(k_g, nk // HEAD_DIM)]).reshape(1, nq + nk)
    cos, sin = _rope_tables(s, with_rope)
    assert n_heads <= LANES
    return pl.pallas_call(
        functools.partial(_qkv_kernel, nq=nq, nk=nk),
        out_shape=(jax.ShapeDtypeStruct((b, s, nq), BF16),
                   jax.ShapeDtypeStruct((b, s, nk), BF16),
                   jax.ShapeDtypeStruct((b, s, nk), BF16)),
        grid=(b, s // t),
        in_specs=[pl.BlockSpec((1, t, d), lambda i, j: (i, j, 0)),
                  pl.BlockSpec((1, 6, d), lambda i, j: (i, 0, 0)),
                  _resident((1, d)), _resident(w_dup.shape),
                  _resident((nq + nk, LANES)), _resident((LANES, nq + nk)), _resident((1, nq + nk)),
                  pl.BlockSpec((t, LANES), lambda i, j: (j, 0)),
                  pl.BlockSpec((t, LANES), lambda i, j: (j, 0))],
        out_specs=(pl.BlockSpec((1, t, nq), lambda i, j: (i, j, 0)),
                   pl.BlockSpec((1, t, nk), lambda i, j: (i, j, 0)),
                   pl.BlockSpec((1, t, nk), lambda i, j: (i, j, 0))),
        compiler_params=_params(2),
        name="qkv",
    )(h, mod, g.reshape(1, d), w_dup, jnp.asarray(ind, BF16), jnp.asarray(ind.T, BF16), hg, cos, sin)


def _attn_kernel(sink_ref, q_ref, *refs, band, s_len, n_ctx):
    if band:
        k_ref, v_ref, kc_ref, vc_ref, o_ref = refs
    else:
        kc_ref, vc_ref, o_ref = refs
    j = pl.program_id(1)
    nq = q_ref.shape[1]
    n_band = 3 * ATTN_BLOCK if band else 0
    n_keys = n_band + n_ctx
    group = q_ref.shape[2] // (KV_HEADS * HEAD_DIM)
    pairs = group // 2
    scale = HEAD_DIM ** -0.5
    lane = lax.broadcasted_iota(jnp.int32, (1, LANES), 1)
    low = lane < HEAD_DIM

    if band:
        start = pl.multiple_of(jnp.clip((j - 1) * ATTN_BLOCK, 0, s_len - n_band), ATTN_BLOCK)
        qpos = j * ATTN_BLOCK + lax.broadcasted_iota(jnp.int32, (nq, n_band), 0)
        kpos = start + lax.broadcasted_iota(jnp.int32, (nq, n_band), 1)
        bias_band = jnp.where(jnp.abs(qpos - kpos) <= WINDOW, 0.0, NEG_BIG).astype(F32)
        bias_half = jnp.concatenate([bias_band, jnp.zeros((nq, n_ctx), F32)], axis=1)
        bias = jnp.concatenate([bias_half, bias_half], axis=1)
        bias = jnp.concatenate([bias] * pairs, axis=0)

    for kv in range(KV_HEADS):
        lanes_kv = slice(kv * LANES, (kv + 1) * LANES)
        k_all = kc_ref[0, :, lanes_kv]
        v_all = vc_ref[0, :, lanes_kv]
        if band:
            k_all = jnp.concatenate([k_ref[0, pl.ds(start, n_band), lanes_kv], k_all], axis=0)
            v_all = jnp.concatenate([v_ref[0, pl.ds(start, n_band), lanes_kv], v_all], axis=0)
        zero = jnp.zeros_like(k_all)
        k_bd = jnp.concatenate([jnp.where(low, k_all, zero), jnp.where(low, zero, k_all)], axis=0)
        v_bd = jnp.concatenate([jnp.where(low, v_all, zero), jnp.where(low, zero, v_all)], axis=0)
        base = kv * pairs
        qs = jnp.concatenate([q_ref[0, :, (base + m) * LANES:(base + m + 1) * LANES] for m in range(pairs)], axis=0)
        sc = _dot_nt(qs, k_bd) * scale
        if band:
            sc = sc + bias
        halves = []
        for half in range(2):
            sh = sc[:, half * n_keys:(half + 1) * n_keys]
            sink = jnp.concatenate(
                [jnp.full((nq, 1), sink_ref[kv * group + 2 * m + half], F32) for m in range(pairs)], axis=0)
            mx = jnp.maximum(jnp.max(sh, axis=1, keepdims=True), sink)
            e = jnp.exp(sh - mx)
            denom = jnp.sum(e, axis=1, keepdims=True) + jnp.exp(sink - mx)
            halves.append((e.astype(BF16), denom))
        p = jnp.concatenate([halves[0][0], halves[1][0]], axis=1)
        o = _dot(p, v_bd)
        o = o / jnp.where(low, halves[0][1], halves[1][1])
        for m in range(pairs):
            o_ref[0, :, (base + m) * LANES:(base + m + 1) * LANES] = o[m * nq:(m + 1) * nq].astype(BF16)


def _attention(q, k, v, kc, vc, sink, band):
    b, s, d = q.shape
    n_ctx = kc.shape[1]
    nkv = kc.shape[2]
    tq = ATTN_BLOCK
    in_specs = [pl.BlockSpec(memory_space=pltpu.SMEM),
                pl.BlockSpec((1, tq, d), lambda i, j: (i, j, 0))]
    args = [sink.astype(F32), q]
    if band:
        in_specs += [pl.BlockSpec((1, s, nkv), lambda i, j: (i, 0, 0))] * 2
        args += [k, v]
    in_specs += [pl.BlockSpec((1, n_ctx, nkv), lambda i, j: (i, 0, 0))] * 2
    args += [kc, vc]
    return pl.pallas_call(
        functools.partial(_attn_kernel, band=band, s_len=s, n_ctx=n_ctx),
        out_shape=jax.ShapeDtypeStruct((b, s, d), BF16),
        grid=(b, s // tq),
        in_specs=in_specs,
        out_specs=pl.BlockSpec((1, tq, d), lambda i, j: (i, j, 0)),
        compiler_params=_params(2),
        name="attn_band" if band else "attn_ctx",
    )(*args)


def _dup_kv_columns(w_qkv, d):
    cols = [w_qkv[:, :d]]
    for part in range(2):
        off = d + part * KV_HEADS * HEAD_DIM
        for hh in range(KV_HEADS):
            blk = w_qkv[:, off + hh * HEAD_DIM: off + (hh + 1) * HEAD_DIM]
            cols += [blk] * (LANES // HEAD_DIM)
    return jnp.concatenate(cols, axis=1).astype(BF16)


def _glu_kernel(h_ref, mod_ref, g_ref, w_ref, b_ref, u_ref):
    n = _norm_mod(h_ref[0], g_ref[...], mod_ref, 0).astype(BF16)
    a = _dot(n, w_ref[...]) + b_ref[...]
    d = u_ref.shape[2]
    u_ref[0] = a[:, :d] * jax.nn.sigmoid(a[:, d:])


def _conv_kernel(u_ref, up_ref, un_ref, dw_ref, dwb_ref, lng_ref, lnb_ref, w_ref, b_ref, h_ref, mod_ref,
                 o_ref, ext_ref, y_ref, *, rows, lane_chunk):
    j = pl.program_id(1)
    t, d = u_ref.shape[1], u_ref.shape[2]
    ext_ref[0:CONV_HALO, :] = jnp.where(j > 0, up_ref[0], 0.0)
    ext_ref[CONV_HALO:CONV_HALO + t, :] = u_ref[0]
    ext_ref[CONV_HALO + t:, :] = jnp.where(j < pl.num_programs(1) - 1, un_ref[0], 0.0)
    shift = CONV_HALO - CONV_WIDTH // 2

    def row_chunk(i, carry):
        r0 = pl.multiple_of(i * rows, rows)
        for c in range(d // lane_chunk):
            ls = slice(c * lane_chunk, (c + 1) * lane_chunk)
            acc = jnp.zeros((rows, lane_chunk), F32)
            for res in range(SUBLANES):
                part = jnp.zeros((rows + SUBLANES, lane_chunk), F32)
                for off in range(res, CONV_WIDTH + shift, SUBLANES):
                    tap = off - shift
                    if 0 <= tap < CONV_WIDTH:
                        part = part + dw_ref[tap:tap + 1, ls] * ext_ref[pl.ds(r0 + off - res, rows + SUBLANES), ls]
                acc = acc + part[res:res + rows]
            y_ref[pl.ds(r0, rows), ls] = acc
        return carry

    lax.fori_loop(0, t // rows, row_chunk, 0)
    y = _layer_norm(y_ref[...] + dwb_ref[...], lng_ref[...], lnb_ref[...])
    y = (y * jax.nn.sigmoid(y)).astype(BF16)
    acc = _dot(y, w_ref[...]) + b_ref[...]
    o_ref[0] = h_ref[0] + mod_ref[0, 2:3, :] * acc


def _conv_module(h, mod, g, w_in, b_in, dw, dw_b, ln_g, ln_b, w_out, b_out):
    b, s, d = h.shape
    t = _row_tile(s)
    u = pl.pallas_call(
        _glu_kernel,
        out_shape=jax.ShapeDtypeStruct((b, s, d), F32),
        grid=(b, s // t),
        in_specs=[pl.BlockSpec((1, t, d), lambda i, j: (i, j, 0)),
                  pl.BlockSpec((1, 6, d), lambda i, j: (i, 0, 0)),
                  _resident((1, d)), _resident((d, 2 * d)), _resident((1, 2 * d))],
        out_specs=pl.BlockSpec((1, t, d), lambda i, j: (i, j, 0)),
        compiler_params=_params(2),
        name="conv_glu",
    )(h, mod, g.reshape(1, d), w_in.astype(BF16), b_in.reshape(1, 2 * d))
    hb = t // CONV_HALO
    last = s // CONV_HALO - 1
    dw_pad = jnp.concatenate([dw, jnp.zeros((32 - CONV_WIDTH, d), F32)], axis=0)
    return pl.pallas_call(
        functools.partial(_conv_kernel, rows=64, lane_chunk=LANES),
        out_shape=jax.ShapeDtypeStruct(h.shape, F32),
        grid=(b, s // t),
        in_specs=[pl.BlockSpec((1, t, d), lambda i, j: (i, j, 0)),
                  pl.BlockSpec((1, CONV_HALO, d), lambda i, j: (i, jnp.maximum(j * hb - 1, 0), 0)),
                  pl.BlockSpec((1, CONV_HALO, d), lambda i, j: (i, jnp.minimum((j + 1) * hb, last), 0)),
                  _resident((32, d)), _resident((1, d)), _resident((1, d)), _resident((1, d)),
                  _resident((d, d)), _resident((1, d)),
                  pl.BlockSpec((1, t, d), lambda i, j: (i, j, 0)),
                  pl.BlockSpec((1, 6, d), lambda i, j: (i, 0, 0))],
        out_specs=pl.BlockSpec((1, t, d), lambda i, j: (i, j, 0)),
        scratch_shapes=[pltpu.VMEM((t + 2 * CONV_HALO, d), F32), pltpu.VMEM((t, d), F32)],
        compiler_params=_params(2),
        name="conv_dw",
    )(u, u, u, dw_pad, dw_b.reshape(1, d), ln_g.reshape(1, d), ln_b.reshape(1, d),
      w_out.astype(BF16), b_out.reshape(1, d), h, mod)


def _gmlp_kernel(h_ref, mod_ref, g_ref, w_in_ref, b_in_ref, lng_ref, lnb_ref, ws_ref, bs_ref, w_out_ref,
                 o_ref, gate_ref):
    x = h_ref[0]
    t, d = x.shape
    n = _norm_mod(x, g_ref[...], mod_ref, 0).astype(BF16)
    z = jax.nn.gelu(_dot(n, w_in_ref[...]) + b_in_ref[...])
    u = z[:, :d]
    v = _layer_norm(z[:, d:], lng_ref[...], lnb_ref[...]).astype(BF16)
    gd = d // GMLP_GROUPS
    for c in range(t // GMLP_CHUNK):
        rs = slice(c * GMLP_CHUNK, (c + 1) * GMLP_CHUNK)
        for gi in range(GMLP_GROUPS):
            ls = slice(gi * gd, (gi + 1) * gd)
            sg = _dot(ws_ref[gi], v[rs, ls]) + jnp.tile(bs_ref[gi], (1, gd // LANES))
            gate_ref[rs, ls] = (u[rs, ls] * sg).astype(BF16)
    o_ref[0] = x + mod_ref[0, 2:3, :] * _dot(gate_ref[...], w_out_ref[...])


def _chunk_gmlp(h, mod, g, w_in, b_in, ln_g, ln_b, w_s, b_s, w_out):
    b, s, d = h.shape
    t = 256 if s % 256 == 0 else GMLP_CHUNK
    bs = jnp.broadcast_to(b_s[:, :, None], (GMLP_GROUPS, GMLP_CHUNK, LANES)).astype(F32)
    return pl.pallas_call(
        _gmlp_kernel,
        out_shape=jax.ShapeDtypeStruct(h.shape, F32),
        grid=(b, s // t),
        in_specs=[pl.BlockSpec((1, t, d), lambda i, j: (i, j, 0)),
                  pl.BlockSpec((1, 6, d), lambda i, j: (i, 0, 0)),
                  _resident((1, d)), _resident((d, 2 * d)), _resident((1, 2 * d)),
                  _resident((1, d)), _resident((1, d)),
                  _resident((GMLP_GROUPS, GMLP_CHUNK, GMLP_CHUNK)), _resident((GMLP_GROUPS, GMLP_CHUNK, LANES)),
                  _resident((d, d))],
        out_specs=pl.BlockSpec((1, t, d), lambda i, j: (i, j, 0)),
        scratch_shapes=[pltpu.VMEM((t, d), BF16)],
        compiler_params=_params(2),
        name="gmlp",
    )(h, mod, g.reshape(1, d), w_in.astype(BF16), b_in.reshape(1, 2 * d), ln_g.reshape(1, d),
      ln_b.reshape(1, d), w_s.astype(BF16), bs, w_out.astype(BF16))


def kernel(x, c, ctx, c_ctx, ada_w, ada_b, norm1_g, norm2_g, mlp_w1, mlp_w2, fnet_w, fnet_b, attn_w_qkv, attn_q_g, attn_k_g, attn_sink, attn_w_o, conv_w_in, conv_b_in, conv_dw, conv_dw_b, conv_ln_g, conv_ln_b, conv_w_out, conv_b_out, gmlp_w_in, gmlp_b_in, gmlp_ln_g, gmlp_ln_b, gmlp_w_s, gmlp_b_s, gmlp_w_out):
    bsz, _, d = x.shape
    depth = ada_w.shape[0]
    n_rows = -(-(bsz + 1) // 8) * 8
    c_rows = jnp.concatenate([c, c_ctx[None, :], jnp.zeros((n_rows - bsz - 1, d), F32)], axis=0)
    mods = _adaln(c_rows, ada_w, ada_b).reshape(depth, n_rows, 6, d)

    h_lat, h_ctx = x, ctx
    for i in range(depth):
        last = i == depth - 1
        kind, li = i % N_MIXERS, i // N_MIXERS
        mod_lat = mods[i, :bsz]
        mod_ctx = jnp.broadcast_to(mods[i, bsz], (bsz, 6, d))
        streams = [(h_lat, mod_lat, True)] + ([] if last else [(h_ctx, mod_ctx, False)])
        outs = []
        if kind == 1:
            w_dup = _dup_kv_columns(attn_w_qkv[li], d)
            w_o = attn_w_o[li].astype(BF16)
            zero_b = jnp.zeros((d,), F32)
            q, k, v = _qkv(h_lat, mod_lat, norm1_g[i], w_dup, attn_q_g[li], attn_k_g[li], True)
            qc, kc, vc = _qkv(h_ctx, mod_ctx, norm1_g[i], w_dup, attn_q_g[li], attn_k_g[li], False)
            y = _attention(q, k, v, kc, vc, attn_sink[li], True)
            outs.append(_proj_res(y, w_o, zero_b, h_lat, mod_lat))
            if not last:
                yc = _attention(qc, None, None, kc, vc, attn_sink[li], False)
                outs.append(_proj_res(yc, w_o, zero_b, h_ctx, mod_ctx))
        else:
            for h, mod, _ in streams:
                if kind == 0:
                    outs.append(_fourier_mixer(h, mod, norm1_g[i], fnet_w[li], fnet_b[li]))
                elif kind == 2:
                    outs.append(_conv_module(h, mod, norm1_g[i], conv_w_in[li], conv_b_in[li], conv_dw[li],
                                             conv_dw_b[li], conv_ln_g[li], conv_ln_b[li], conv_w_out[li],
                                             conv_b_out[li]))
                else:
                    outs.append(_chunk_gmlp(h, mod, norm1_g[i], gmlp_w_in[li], gmlp_b_in[li], gmlp_ln_g[li],
                                            gmlp_ln_b[li], gmlp_w_s[li], gmlp_b_s[li], gmlp_w_out[li]))
        w1 = mlp_w1[i].astype(BF16)
        w2 = mlp_w2[i].astype(BF16)
        h_lat = _mlp(outs[0], mod_lat, norm2_g[i], w1, w2)
        if not last:
            h_ctx = _mlp(outs[1], mod_ctx, norm2_g[i], w1, w2)
    return h_lat
```

```python
import functools
import math

import numpy as np
import jax
import jax.numpy as jnp
from jax import lax
from jax.experimental import pallas as pl
from jax.experimental.pallas import tpu as pltpu

F32 = jnp.float32
BF16 = jnp.bfloat16

NORM_EPS = 1e-6
GRID_W = 64
FNET_GROUPS = 4
HEAD_DIM = 64
KV_HEADS = 2
WINDOW = 128
ATTN_BLOCK = 128
ROPE_THETA = 10000.0
ROPE_PAIRS = HEAD_DIM // 4
CONV_WIDTH = 31
CONV_HALO = 16
GMLP_CHUNK = 128
GMLP_GROUPS = 4
N_MIXERS = 4
LANES = 128
SUBLANES = 8
VMEM_LIMIT = 56 * 1024 * 1024
NEG_BIG = -1e30
LOG2E = math.log2(math.e)


def _resident(shape):
    nd = len(shape)
    return pl.BlockSpec(shape, lambda *_: (0,) * nd, pipeline_mode=pl.Buffered(1))


def _params(n_grid):
    return pltpu.CompilerParams(dimension_semantics=("parallel",) * n_grid, vmem_limit_bytes=VMEM_LIMIT)


def _row_tile(s):
    return 512 if s % 512 == 0 else s


def _dot(a, b):
    return jnp.dot(a, b, preferred_element_type=F32)


def _dot_nt(a, b):
    return lax.dot_general(a, b, (((1,), (1,)), ((), ())), preferred_element_type=F32)


def _norm_mod(x, g, mod_ref, row):
    ms = jnp.mean(x * x, axis=-1, keepdims=True)
    y = x * lax.rsqrt(ms + NORM_EPS) * g
    return y * (1.0 + mod_ref[0, row + 1:row + 2, :]) + mod_ref[0, row:row + 1, :]


def _layer_norm(x, g, b):
    mu = jnp.mean(x, axis=-1, keepdims=True)
    xc = x - mu
    var = jnp.mean(xc * xc, axis=-1, keepdims=True)
    return xc * lax.rsqrt(var + NORM_EPS) * g + b


def _adaln_kernel(c_ref, w_ref, b_ref, o_ref):
    c = c_ref[...]
    s = (c * jax.nn.sigmoid(c)).astype(BF16)
    o_ref[0] = _dot(s, w_ref[0].astype(BF16)) + b_ref[0]


def _adaln(c_rows, ada_w, ada_b):
    depth, d, n = ada_w.shape
    r = c_rows.shape[0]
    tn = 1024
    return pl.pallas_call(
        _adaln_kernel,
        out_shape=jax.ShapeDtypeStruct((depth, r, n), F32),
        grid=(depth, n // tn),
        in_specs=[pl.BlockSpec((r, d), lambda l, j: (0, 0)),
                  pl.BlockSpec((1, d, tn), lambda l, j: (l, 0, j)),
                  pl.BlockSpec((1, 1, tn), lambda l, j: (l, 0, j))],
        out_specs=pl.BlockSpec((1, r, tn), lambda l, j: (l, 0, j)),
        compiler_params=_params(2),
        name="adaln",
    )(c_rows, ada_w, ada_b.reshape(depth, 1, n))


def _mlp_kernel(h_ref, mod_ref, g_ref, w1_ref, w2_ref, o_ref, *, ff_chunk):
    x = h_ref[0]
    n = _norm_mod(x, g_ref[...], mod_ref, 3).astype(BF16)
    d_ff = w1_ref.shape[1]
    acc = jnp.zeros(x.shape, F32)
    for c in range(d_ff // ff_chunk):
        a = _dot(n, w1_ref[:, c * ff_chunk:(c + 1) * ff_chunk])
        a = jnp.square(jnp.maximum(a, 0.0)).astype(BF16)
        acc = acc + _dot(a, w2_ref[c * ff_chunk:(c + 1) * ff_chunk, :])
    o_ref[0] = x + mod_ref[0, 5:6, :] * acc


def _mlp(h, mod, g, w1, w2):
    b, s, d = h.shape
    t = _row_tile(s)
    d_ff = w1.shape[1]
    return pl.pallas_call(
        functools.partial(_mlp_kernel, ff_chunk=1024),
        out_shape=jax.ShapeDtypeStruct(h.shape, F32),
        grid=(b, s // t),
        in_specs=[pl.BlockSpec((1, t, d), lambda i, j: (i, j, 0)),
                  pl.BlockSpec((1, 6, d), lambda i, j: (i, 0, 0)),
                  _resident((1, d)), _resident((d, d_ff)), _resident((d_ff, d))],
        out_specs=pl.BlockSpec((1, t, d), lambda i, j: (i, j, 0)),
        compiler_params=_params(2),
        name="mlp",
    )(h, mod, g.reshape(1, d), w1, w2)


def _proj_res_kernel(y_ref, w_ref, b_ref, h_ref, mod_ref, o_ref):
    acc = _dot(y_ref[0], w_ref[...]) + b_ref[...]
    o_ref[0] = h_ref[0] + mod_ref[0, 2:3, :] * acc


def _proj_res(y, w, bias, h, mod):
    b, s, d = h.shape
    k = y.shape[-1]
    t = _row_tile(s)
    return pl.pallas_call(
        _proj_res_kernel,
        out_shape=jax.ShapeDtypeStruct(h.shape, F32),
        grid=(b, s // t),
        in_specs=[pl.BlockSpec((1, t, k), lambda i, j: (i, j, 0)),
                  _resident((k, d)), _resident((1, d)),
                  pl.BlockSpec((1, t, d), lambda i, j: (i, j, 0)),
                  pl.BlockSpec((1, 6, d), lambda i, j: (i, 0, 0))],
        out_specs=pl.BlockSpec((1, t, d), lambda i, j: (i, j, 0)),
        compiler_params=_params(2),
        name="proj_res",
    )(y, w, bias.reshape(1, d), h, mod)


def _fnet_chan_kernel(h_ref, mod_ref, g_ref, cs_ref, p_ref, q_ref, *, gc):
    n = _norm_mod(h_ref[0], g_ref[...], mod_ref, 0).astype(BF16)
    for gi in range(n.shape[1] // gc):
        r = _dot(n[:, gi * gc:(gi + 1) * gc], cs_ref[...])
        p_ref[0, :, gi * gc:(gi + 1) * gc] = r[:, :gc].astype(BF16)
        q_ref[0, :, gi * gc:(gi + 1) * gc] = r[:, gc:].astype(BF16)


def _fnet_pos_kernel(p_ref, q_ref, tw_ref, csm_ref, o_ref, y_ref, x_ref, *, radix, m, rc):
    tc = p_ref.shape[2]
    if radix == 1:
        y_ref[0, 0:m, :] = p_ref[0]
        y_ref[0, m:2 * m, :] = -q_ref[0]
        o_ref[0] = _dot(csm_ref[...], y_ref[0]).astype(BF16)
        return

    def chunk(i, carry):
        r0 = pl.multiple_of(i * rc, rc)
        a = [p_ref[0, pl.ds(q * m + r0, rc), :].astype(F32) for q in range(4)]
        b = [-q_ref[0, pl.ds(q * m + r0, rc), :].astype(F32) for q in range(4)]
        t0r, t0i = a[0] + a[2], b[0] + b[2]
        t1r, t1i = a[0] - a[2], b[0] - b[2]
        t2r, t2i = a[1] + a[3], b[1] + b[3]
        t3r, t3i = a[1] - a[3], b[1] - b[3]
        bs = [(t0r + t2r, t0i + t2i), (t1r + t3i, t1i - t3r),
              (t0r - t2r, t0i - t2i), (t1r - t3i, t1i + t3r)]
        for pp, (br, bi) in enumerate(bs):
            if pp == 0:
                yr, yi = br, bi
            else:
                c = jnp.tile(tw_ref[pp - 1, 0, pl.ds(r0, rc), :], (1, tc // LANES))
                s = jnp.tile(tw_ref[pp - 1, 1, pl.ds(r0, rc), :], (1, tc // LANES))
                yr = c * br + s * bi
                yi = c * bi - s * br
            y_ref[pp, pl.ds(r0, rc), :] = yr.astype(BF16)
            y_ref[pp, pl.ds(m + r0, rc), :] = yi.astype(BF16)
        return carry

    lax.fori_loop(0, m // rc, chunk, 0)
    for pp in range(radix):
        x = _dot(csm_ref[...], y_ref[pp])
        for cb in range(tc // LANES):
            x_ref[cb, pl.ds(pp, m, stride=radix), :] = x[:, cb * LANES:(cb + 1) * LANES]
    for cb in range(tc // LANES):
        o_ref[0, :, cb * LANES:(cb + 1) * LANES] = x_ref[cb].astype(BF16)


def _fnet_tables(s, gc):
    radix = 4 if s % 512 == 0 else 1
    m = s // radix
    ang_c = 2.0 * np.pi * np.outer(np.arange(gc), np.arange(gc)) / gc
    cs = np.concatenate([np.cos(ang_c), np.sin(ang_c)], axis=1)
    ang_m = 2.0 * np.pi * np.outer(np.arange(m), np.arange(m)) / m
    csm = np.concatenate([np.cos(ang_m), np.sin(ang_m)], axis=1) / math.sqrt(s * gc)
    if radix == 4:
        ang_t = 2.0 * np.pi * np.outer(np.arange(1, 4), np.arange(m)) / s
        tw = np.stack([np.cos(ang_t), np.sin(ang_t)], axis=1)
        tw = np.broadcast_to(tw[..., None], (3, 2, m, LANES))
    else:
        tw = np.zeros((1, 2, 8, LANES))
    return (radix, m, jnp.asarray(cs, BF16), jnp.asarray(csm, BF16), jnp.asarray(tw, F32))


def _fourier_mixer(h, mod, g, w_out, b_out):
    b, s, d = h.shape
    gc = d // FNET_GROUPS
    t = _row_tile(s)
    radix, m, cs, csm, tw = _fnet_tables(s, gc)
    p, q = pl.pallas_call(
        functools.partial(_fnet_chan_kernel, gc=gc),
        out_shape=(jax.ShapeDtypeStruct((b, s, d), BF16),) * 2,
        grid=(b, s // t),
        in_specs=[pl.BlockSpec((1, t, d), lambda i, j: (i, j, 0)),
                  pl.BlockSpec((1, 6, d), lambda i, j: (i, 0, 0)),
                  _resident((1, d)), _resident((gc, 2 * gc))],
        out_specs=(pl.BlockSpec((1, t, d), lambda i, j: (i, j, 0)),) * 2,
        compiler_params=_params(2),
        name="fnet_chan",
    )(h, mod, g.reshape(1, d), cs)
    tc = 256
    nct = d // tc
    mixed = pl.pallas_call(
        functools.partial(_fnet_pos_kernel, radix=radix, m=m, rc=min(m, 128)),
        out_shape=jax.ShapeDtypeStruct((b, s, d), BF16),
        grid=(b, d // tc),
        in_specs=[pl.BlockSpec((1, s, tc), lambda i, j: (i, 0, j)),
                  pl.BlockSpec((1, s, tc), lambda i, j: (i, 0, j)),
                  _resident(tw.shape), _resident((m, 2 * m))],
        out_specs=pl.BlockSpec((1, s, tc), lambda i, j: (i, 0, j)),
        scratch_shapes=[pltpu.VMEM((radix, 2 * m, tc), BF16),
                        pltpu.VMEM((tc // LANES, s if radix > 1 else SUBLANES, LANES), F32)],
        compiler_params=_params(2),
        name="fnet_pos",
    )(p, q, tw, csm)
    return _proj_res(mixed, w_out.astype(BF16), b_out, h, mod)


def _qkv_kernel(h_ref, mod_ref, g_ref, w_ref, ind_ref, exp_ref, hg_ref, cos_ref, sin_ref,
                q_ref, k_ref, v_ref, *, nq, nk):
    n = _norm_mod(h_ref[0], g_ref[...], mod_ref, 0).astype(BF16)
    acc = _dot(n, w_ref[...])
    qk = acc[:, :nq + nk]
    ssq = _dot((qk * qk).astype(BF16), ind_ref[...])
    r = lax.rsqrt(ssq * (1.0 / HEAD_DIM) + NORM_EPS)
    r_hi = r.astype(BF16)
    r_lo = (r - r_hi.astype(F32)).astype(BF16)
    rexp = _dot(r_hi, exp_ref[...]) + _dot(r_lo, exp_ref[...])
    qk = qk * rexp * hg_ref[...]
    lane = lax.broadcasted_iota(jnp.int32, (1, LANES), 1)
    first_half = (lane % (2 * ROPE_PAIRS)) < ROPE_PAIRS
    cos = cos_ref[...]
    sin = sin_ref[...]
    for c in range((nq + nk) // LANES):
        x = qk[:, c * LANES:(c + 1) * LANES]
        partner = jnp.where(first_half, pltpu.roll(x, LANES - ROPE_PAIRS, 1), pltpu.roll(x, ROPE_PAIRS, 1))
        y = (x * cos + partner * sin).astype(BF16)
        if c < nq // LANES:
            q_ref[0, :, c * LANES:(c + 1) * LANES] = y
        else:
            k_ref[0, :, c * LANES - nq:(c + 1) * LANES - nq] = y
    v_ref[0] = acc[:, nq + nk:].astype(BF16)


def _rope_tables(s, with_rope):
    if not with_rope:
        return jnp.ones((s, LANES), F32), jnp.zeros((s, LANES), F32)
    rows = s // GRID_W
    row = jnp.repeat(jnp.arange(rows, dtype=F32), GRID_W)
    col = jnp.tile(jnp.arange(GRID_W, dtype=F32), rows)
    inv = ROPE_THETA ** (-jnp.arange(ROPE_PAIRS, dtype=F32) / ROPE_PAIRS)
    ang_r = row[:, None] * inv[None, :]
    ang_c = col[:, None] * inv[None, :]
    cos = jnp.concatenate([jnp.cos(ang_r)] * 2 + [jnp.cos(ang_c)] * 2, axis=1)
    sin = jnp.concatenate([-jnp.sin(ang_r), jnp.sin(ang_r), -jnp.sin(ang_c), jnp.sin(ang_c)], axis=1)
    return jnp.tile(cos, (1, LANES // HEAD_DIM)), jnp.tile(sin, (1, LANES // HEAD_DIM))


def _qkv(h, mod, g, w_dup, q_g, k_g, with_rope):
    b, s, d = h.shape
    nq = d
    nk = KV_HEADS * LANES
    t = _row_tile(s)
    n_heads = (nq + nk) // HEAD_DIM
    ind = np.zeros((nq + nk, LANES), np.float32)
    ind[np.arange(nq + nk), np.arange(nq + nk) // HEAD_DIM] = 1.0
    q_gain = q_g * (HEAD_DIM ** -0.5 * LOG2E)
    hg = jnp.concatenate([jnp.tile(q_gain, nq // HEAD_DIM), jnp.tile(k_g, nk // HEAD_DIM)]).reshape(1, nq + nk)
    cos, sin = _rope_tables(s, with_rope)
    assert n_heads <= LANES
    return pl.pallas_call(
        functools.partial(_qkv_kernel, nq=nq, nk=nk),
        out_shape=(jax.ShapeDtypeStruct((b, s, nq), BF16),
                   jax.ShapeDtypeStruct((b, s, nk), BF16),
                   jax.ShapeDtypeStruct((b, s, nk), BF16)),
        grid=(b, s // t),
        in_specs=[pl.BlockSpec((1, t, d), lambda i, j: (i, j, 0)),
                  pl.BlockSpec((1, 6, d), lambda i, j: (i, 0, 0)),
                  _resident((1, d)), _resident(w_dup.shape),
                  _resident((nq + nk, LANES)), _resident((LANES, nq + nk)), _resident((1, nq + nk)),
                  pl.BlockSpec((t, LANES), lambda i, j: (j, 0)),
                  pl.BlockSpec((t, LANES), lambda i, j: (j, 0))],
        out_specs=(pl.BlockSpec((1, t, nq), lambda i, j: (i, j, 0)),
                   pl.BlockSpec((1, t, nk), lambda i, j: (i, j, 0)),
                   pl.BlockSpec((1, t, nk), lambda i, j: (i, j, 0))),
        compiler_params=_params(2),
        name="qkv",
    )(h, mod, g.reshape(1, d), w_dup, jnp.asarray(ind, BF16), jnp.asarray(ind.T, BF16), hg, cos, sin)


def _attn_kernel(sink_ref, q_ref, *refs, band, s_len, n_ctx):
    if band:
        k_ref, v_ref, kc_ref, vc_ref, o_ref = refs
    else:
        kc_ref, vc_ref, o_ref = refs
    j = pl.program_id(1)
    nq = q_ref.shape[1]
    n_band = 3 * ATTN_BLOCK if band else 0
    n_keys = n_band + n_ctx
    group = q_ref.shape[2] // (KV_HEADS * HEAD_DIM)
    pairs = group // 2
    lane = lax.broadcasted_iota(jnp.int32, (1, LANES), 1)
    low = lane < HEAD_DIM

    if band:
        start = pl.multiple_of(jnp.clip((j - 1) * ATTN_BLOCK, 0, s_len - n_band), ATTN_BLOCK)
        qpos = j * ATTN_BLOCK + lax.broadcasted_iota(jnp.int32, (nq, n_band), 0)
        kpos = start + lax.broadcasted_iota(jnp.int32, (nq, n_band), 1)
        bias_band = jnp.where(jnp.abs(qpos - kpos) <= WINDOW, 0.0, NEG_BIG).astype(F32)

    for kv in range(KV_HEADS):
        lanes_kv = slice(kv * LANES, (kv + 1) * LANES)
        k_all = kc_ref[0, :, lanes_kv]
        v_all = vc_ref[0, :, lanes_kv]
        if band:
            k_all = jnp.concatenate([k_ref[0, pl.ds(start, n_band), lanes_kv], k_all], axis=0)
            v_all = jnp.concatenate([v_ref[0, pl.ds(start, n_band), lanes_kv], v_all], axis=0)
        zero = jnp.zeros_like(k_all)
        k_bd = jnp.concatenate([jnp.where(low, k_all, zero), jnp.where(low, zero, k_all)], axis=0)
        v_bd = jnp.concatenate([jnp.where(low, v_all, zero), jnp.where(low, zero, v_all)], axis=0)
        for m in range(pairs):
            blk = slice((kv * pairs + m) * LANES, (kv * pairs + m + 1) * LANES)
            sc = _dot_nt(q_ref[0, :, blk], k_bd)
            probs, denoms = [], []
            for half in range(2):
                sh = sc[:, half * n_keys:(half + 1) * n_keys]
                if band:
                    sh = jnp.concatenate([sh[:, :n_band] + bias_band, sh[:, n_band:]], axis=1)
                sink = sink_ref[kv * group + 2 * m + half] * LOG2E
                mx = jnp.maximum(jnp.max(sh, axis=1, keepdims=True), sink)
                e = jnp.exp2(sh - mx)
                denoms.append(jnp.sum(e, axis=1, keepdims=True) + jnp.exp2(sink - mx))
                probs.append(e.astype(BF16))
            o = _dot(jnp.concatenate(probs, axis=1), v_bd)
            o_ref[0, :, blk] = (o / jnp.where(low, denoms[0], denoms[1])).astype(BF16)


def _attention(q, k, v, kc, vc, sink, band):
    b, s, d = q.shape
    n_ctx = kc.shape[1]
    nkv = kc.shape[2]
    tq = ATTN_BLOCK
    in_specs = [pl.BlockSpec(memory_space=pltpu.SMEM),
                pl.BlockSpec((1, tq, d), lambda i, j: (i, j, 0))]
    args = [sink.astype(F32), q]
    if band:
        in_specs += [pl.BlockSpec((1, s, nkv), lambda i, j: (i, 0, 0))] * 2
        args += [k, v]
    in_specs += [pl.BlockSpec((1, n_ctx, nkv), lambda i, j: (i, 0, 0))] * 2
    args += [kc, vc]
    return pl.pallas_call(
        functools.partial(_attn_kernel, band=band, s_len=s, n_ctx=n_ctx),
        out_shape=jax.ShapeDtypeStruct((b, s, d), BF16),
        grid=(b, s // tq),
        in_specs=in_specs,
        out_specs=pl.BlockSpec((1, tq, d), lambda i, j: (i, j, 0)),
        compiler_params=_params(2),
        name="attn_band" if band else "attn_ctx",
    )(*args)


def _dup_kv_columns(w_qkv, d):
    cols = [w_qkv[:, :d]]
    for part in range(2):
        off = d + part * KV_HEADS * HEAD_DIM
        for hh in range(KV_HEADS):
            blk = w_qkv[:, off + hh * HEAD_DIM: off + (hh + 1) * HEAD_DIM]
            cols += [blk] * (LANES // HEAD_DIM)
    return jnp.concatenate(cols, axis=1).astype(BF16)


def _glu_kernel(h_ref, mod_ref, g_ref, w_ref, b_ref, u_ref):
    n = _norm_mod(h_ref[0], g_ref[...], mod_ref, 0).astype(BF16)
    a = _dot(n, w_ref[...]) + b_ref[...]
    d = u_ref.shape[2]
    u_ref[0] = a[:, :d] * jax.nn.sigmoid(a[:, d:])


def _conv_kernel(u_ref, up_ref, un_ref, dw_ref, dwb_ref, lng_ref, lnb_ref, w_ref, b_ref, h_ref, mod_ref,
                 o_ref, ext_ref, y_ref, *, rows):
    j = pl.program_id(1)
    t, d = u_ref.shape[1], u_ref.shape[2]
    n_chunks = d // LANES
    for c in range(n_chunks):
        ls = slice(c * LANES, (c + 1) * LANES)
        ext_ref[c, 0:CONV_HALO, :] = jnp.where(j > 0, up_ref[0, :, ls], 0.0)
        ext_ref[c, CONV_HALO:CONV_HALO + t, :] = u_ref[0, :, ls]
        ext_ref[c, CONV_HALO + t:, :] = jnp.where(j < pl.num_programs(1) - 1, un_ref[0, :, ls], 0.0)
    shift = CONV_HALO - CONV_WIDTH // 2

    def lane_chunk(c, carry):
        for r0 in range(0, t, rows):
            acc = jnp.zeros((rows, LANES), F32)
            for tap in range(CONV_WIDTH):
                acc = acc + dw_ref[c, tap:tap + 1, :] * ext_ref[c, r0 + tap + shift:r0 + tap + shift + rows, :]
            y_ref[c, r0:r0 + rows, :] = acc
        return carry

    lax.fori_loop(0, n_chunks, lane_chunk, 0)
    y = jnp.concatenate([y_ref[c] for c in range(n_chunks)], axis=1)
    y = _layer_norm(y + dwb_ref[...], lng_ref[...], lnb_ref[...])
    y = (y * jax.nn.sigmoid(y)).astype(BF16)
    acc = _dot(y, w_ref[...]) + b_ref[...]
    o_ref[0] = h_ref[0] + mod_ref[0, 2:3, :] * acc


def _conv_module(h, mod, g, w_in, b_in, dw, dw_b, ln_g, ln_b, w_out, b_out):
    b, s, d = h.shape
    t = _row_tile(s)
    u = pl.pallas_call(
        _glu_kernel,
        out_shape=jax.ShapeDtypeStruct((b, s, d), F32),
        grid=(b, s // t),
        in_specs=[pl.BlockSpec((1, t, d), lambda i, j: (i, j, 0)),
                  pl.BlockSpec((1, 6, d), lambda i, j: (i, 0, 0)),
                  _resident((1, d)), _resident((d, 2 * d)), _resident((1, 2 * d))],
        out_specs=pl.BlockSpec((1, t, d), lambda i, j: (i, j, 0)),
        compiler_params=_params(2),
        name="conv_glu",
    )(h, mod, g.reshape(1, d), w_in.astype(BF16), b_in.reshape(1, 2 * d))
    hb = t // CONV_HALO
    last = s // CONV_HALO - 1
    n_chunks = d // LANES
    taps_pad = -(-CONV_WIDTH // SUBLANES) * SUBLANES
    dw_pad = jnp.concatenate([dw, jnp.zeros((taps_pad - CONV_WIDTH, d), F32)], axis=0)
    dw_pad = dw_pad.reshape(taps_pad, n_chunks, LANES).transpose(1, 0, 2)
    return pl.pallas_call(
        functools.partial(_conv_kernel, rows=64),
        out_shape=jax.ShapeDtypeStruct(h.shape, F32),
        grid=(b, s // t),
        in_specs=[pl.BlockSpec((1, t, d), lambda i, j: (i, j, 0)),
                  pl.BlockSpec((1, CONV_HALO, d), lambda i, j: (i, jnp.maximum(j * hb - 1, 0), 0)),
                  pl.BlockSpec((1, CONV_HALO, d), lambda i, j: (i, jnp.minimum((j + 1) * hb, last), 0)),
                  _resident((n_chunks, taps_pad, LANES)), _resident((1, d)), _resident((1, d)), _resident((1, d)),
                  _resident((d, d)), _resident((1, d)),
                  pl.BlockSpec((1, t, d), lambda i, j: (i, j, 0)),
                  pl.BlockSpec((1, 6, d), lambda i, j: (i, 0, 0))],
        out_specs=pl.BlockSpec((1, t, d), lambda i, j: (i, j, 0)),
        scratch_shapes=[pltpu.VMEM((n_chunks, t + 2 * CONV_HALO, LANES), F32),
                        pltpu.VMEM((n_chunks, t, LANES), F32)],
        compiler_params=_params(2),
        name="conv_dw",
    )(u, u, u, dw_pad, dw_b.reshape(1, d), ln_g.reshape(1, d), ln_b.reshape(1, d),
      w_out.astype(BF16), b_out.reshape(1, d), h, mod)


def _gmlp_kernel(h_ref, mod_ref, g_ref, w_in_ref, b_in_ref, lng_ref, lnb_ref, ws_ref, bs_ref, w_out_ref,
                 o_ref, gate_ref):
    x = h_ref[0]
    t, d = x.shape
    n = _norm_mod(x, g_ref[...], mod_ref, 0).astype(BF16)
    z = jax.nn.gelu(_dot(n, w_in_ref[...]) + b_in_ref[...])
    u = z[:, :d]
    v = _layer_norm(z[:, d:], lng_ref[...], lnb_ref[...]).astype(BF16)
    gd = d // GMLP_GROUPS
    for c in range(t // GMLP_CHUNK):
        rs = slice(c * GMLP_CHUNK, (c + 1) * GMLP_CHUNK)
        for gi in range(GMLP_GROUPS):
            ls = slice(gi * gd, (gi + 1) * gd)
            sg = _dot(ws_ref[gi], v[rs, ls]) + jnp.tile(bs_ref[gi], (1, gd // LANES))
            gate_ref[rs, ls] = (u[rs, ls] * sg).astype(BF16)
    o_ref[0] = x + mod_ref[0, 2:3, :] * _dot(gate_ref[...], w_out_ref[...])


def _chunk_gmlp(h, mod, g, w_in, b_in, ln_g, ln_b, w_s, b_s, w_out):
    b, s, d = h.shape
    t = 256 if s % 256 == 0 else GMLP_CHUNK
    bs = jnp.broadcast_to(b_s[:, :, None], (GMLP_GROUPS, GMLP_CHUNK, LANES)).astype(F32)
    return pl.pallas_call(
        _gmlp_kernel,
        out_shape=jax.ShapeDtypeStruct(h.shape, F32),
        grid=(b, s // t),
        in_specs=[pl.BlockSpec((1, t, d), lambda i, j: (i, j, 0)),
                  pl.BlockSpec((1, 6, d), lambda i, j: (i, 0, 0)),
                  _resident((1, d)), _resident((d, 2 * d)), _resident((1, 2 * d)),
                  _resident((1, d)), _resident((1, d)),
                  _resident((GMLP_GROUPS, GMLP_CHUNK, GMLP_CHUNK)), _resident((GMLP_GROUPS, GMLP_CHUNK, LANES)),
                  _resident((d, d))],
        out_specs=pl.BlockSpec((1, t, d), lambda i, j: (i, j, 0)),
        scratch_shapes=[pltpu.VMEM((t, d), BF16)],
        compiler_params=_params(2),
        name="gmlp",
    )(h, mod, g.reshape(1, d), w_in.astype(BF16), b_in.reshape(1, 2 * d), ln_g.reshape(1, d),
      ln_b.reshape(1, d), w_s.astype(BF16), bs, w_out.astype(BF16))


def kernel(x, c, ctx, c_ctx, ada_w, ada_b, norm1_g, norm2_g, mlp_w1, mlp_w2, fnet_w, fnet_b, attn_w_qkv, attn_q_g, attn_k_g, attn_sink, attn_w_o, conv_w_in, conv_b_in, conv_dw, conv_dw_b, conv_ln_g, conv_ln_b, conv_w_out, conv_b_out, gmlp_w_in, gmlp_b_in, gmlp_ln_g, gmlp_ln_b, gmlp_w_s, gmlp_b_s, gmlp_w_out):
    bsz, _, d = x.shape
    depth = ada_w.shape[0]
    n_rows = -(-(bsz + 1) // 8) * 8
    c_rows = jnp.concatenate([c, c_ctx[None, :], jnp.zeros((n_rows - bsz - 1, d), F32)], axis=0)
    mods = _adaln(c_rows, ada_w, ada_b).reshape(depth, n_rows, 6, d)

    h_lat, h_ctx = x, ctx
    for i in range(depth):
        last = i == depth - 1
        kind, li = i % N_MIXERS, i // N_MIXERS
        mod_lat = mods[i, :bsz]
        mod_ctx = jnp.broadcast_to(mods[i, bsz], (bsz, 6, d))
        streams = [(h_lat, mod_lat, True)] + ([] if last else [(h_ctx, mod_ctx, False)])
        outs = []
        if kind == 1:
            w_dup = _dup_kv_columns(attn_w_qkv[li], d)
            w_o = attn_w_o[li].astype(BF16)
            zero_b = jnp.zeros((d,), F32)
            q, k, v = _qkv(h_lat, mod_lat, norm1_g[i], w_dup, attn_q_g[li], attn_k_g[li], True)
            qc, kc, vc = _qkv(h_ctx, mod_ctx, norm1_g[i], w_dup, attn_q_g[li], attn_k_g[li], False)
            y = _attention(q, k, v, kc, vc, attn_sink[li], True)
            outs.append(_proj_res(y, w_o, zero_b, h_lat, mod_lat))
            if not last:
                yc = _attention(qc, None, None, kc, vc, attn_sink[li], False)
                outs.append(_proj_res(yc, w_o, zero_b, h_ctx, mod_ctx))
        else:
            for h, mod, _ in streams:
                if kind == 0:
                    outs.append(_fourier_mixer(h, mod, norm1_g[i], fnet_w[li], fnet_b[li]))
                elif kind == 2:
                    outs.append(_conv_module(h, mod, norm1_g[i], conv_w_in[li], conv_b_in[li], conv_dw[li],
                                             conv_dw_b[li], conv_ln_g[li], conv_ln_b[li], conv_w_out[li],
                                             conv_b_out[li]))
                else:
                    outs.append(_chunk_gmlp(h, mod, norm1_g[i], gmlp_w_in[li], gmlp_b_in[li], gmlp_ln_g[li],
                                            gmlp_ln_b[li], gmlp_w_s[li], gmlp_b_s[li], gmlp_w_out[li]))
        w1 = mlp_w1[i].astype(BF16)
        w2 = mlp_w2[i].astype(BF16)
        h_lat = _mlp(outs[0], mod_lat, norm2_g[i], w1, w2)
        if not last:
            h_ctx = _mlp(outs[1], mod_ctx, norm2_g[i], w1, w2)
    return h_lat
```

```python
import functools
import math

import numpy as np
import jax
import jax.numpy as jnp
from jax import lax
from jax.experimental import pallas as pl
from jax.experimental.pallas import tpu as pltpu

F32 = jnp.float32
BF16 = jnp.bfloat16

NORM_EPS = 1e-6
GRID_W = 64
FNET_GROUPS = 4
HEAD_DIM = 64
KV_HEADS = 2
WINDOW = 128
ATTN_BLOCK = 128
ROPE_THETA = 10000.0
ROPE_PAIRS = HEAD_DIM // 4
CONV_WIDTH = 31
CONV_HALO = 16
GMLP_CHUNK = 128
GMLP_GROUPS = 4
N_MIXERS = 4
LANES = 128
SUBLANES = 8
VMEM_LIMIT = 56 * 1024 * 1024
NEG_BIG = -1e30
LOG2E = math.log2(math.e)


def _resident(shape):
    nd = len(shape)
    return pl.BlockSpec(shape, lambda *_: (0,) * nd, pipeline_mode=pl.Buffered(1))


def _params(n_grid):
    return pltpu.CompilerParams(dimension_semantics=("parallel",) * n_grid, vmem_limit_bytes=VMEM_LIMIT)


def _row_tile(s):
    return 512 if s % 512 == 0 else s


def _wide_row_tile(s):
    return 1024 if s % 1024 == 0 else _row_tile(s)


def _dot(a, b):
    return jnp.dot(a, b, preferred_element_type=F32)


def _dot_nt(a, b):
    return lax.dot_general(a, b, (((1,), (1,)), ((), ())), preferred_element_type=F32)


def _norm_mod(x, g, mod_ref, row):
    ms = jnp.mean(x * x, axis=-1, keepdims=True)
    y = x * lax.rsqrt(ms + NORM_EPS) * g
    return y * (1.0 + mod_ref[0, row + 1:row + 2, :]) + mod_ref[0, row:row + 1, :]


def _pipeline_rows(t, sub, project, finish):
    acc = project(0)
    for r0 in range(0, t, sub):
        nxt = project(r0 + sub) if r0 + sub < t else None
        finish(r0, acc)
        acc = nxt


def _layer_norm(x, g, b):
    mu = jnp.mean(x, axis=-1, keepdims=True)
    xc = x - mu
    var = jnp.mean(xc * xc, axis=-1, keepdims=True)
    return xc * lax.rsqrt(var + NORM_EPS) * g + b


def _adaln_kernel(c_ref, w_ref, b_ref, o_ref):
    c = c_ref[...]
    s = (c * jax.nn.sigmoid(c)).astype(BF16)
    o_ref[0] = _dot(s, w_ref[0].astype(BF16)) + b_ref[0]


def _adaln(c_rows, ada_w, ada_b):
    depth, d, n = ada_w.shape
    r = c_rows.shape[0]
    tn = 1024
    return pl.pallas_call(
        _adaln_kernel,
        out_shape=jax.ShapeDtypeStruct((depth, r, n), F32),
        grid=(depth, n // tn),
        in_specs=[pl.BlockSpec((r, d), lambda l, j: (0, 0)),
                  pl.BlockSpec((1, d, tn), lambda l, j: (l, 0, j)),
                  pl.BlockSpec((1, 1, tn), lambda l, j: (l, 0, j))],
        out_specs=pl.BlockSpec((1, r, tn), lambda l, j: (l, 0, j)),
        compiler_params=_params(2),
        name="adaln",
    )(c_rows, ada_w, ada_b.reshape(depth, 1, n))


def _mlp_kernel(h_ref, mod_ref, g_ref, w1_ref, w2_ref, *refs, ff_chunk):
    x = h_ref[0]
    if len(refs) == 4:
        y_ref, wo_ref, bo_ref, o_ref = refs
        x = x + mod_ref[0, 2:3, :] * (_dot(y_ref[0], wo_ref[...]) + bo_ref[...])
    else:
        o_ref, = refs
    n = _norm_mod(x, g_ref[...], mod_ref, 3).astype(BF16)
    d_ff = w1_ref.shape[1]
    acc = jnp.zeros(x.shape, F32)
    for c in range(d_ff // ff_chunk):
        a = _dot(n, w1_ref[:, c * ff_chunk:(c + 1) * ff_chunk])
        a = jnp.square(jnp.maximum(a, 0.0)).astype(BF16)
        acc = acc + _dot(a, w2_ref[c * ff_chunk:(c + 1) * ff_chunk, :])
    o_ref[0] = x + mod_ref[0, 5:6, :] * acc


def _mlp(h, mod, g, w1, w2, pending=None):
    b, s, d = h.shape
    t = _row_tile(s)
    d_ff = w1.shape[1]
    in_specs = [pl.BlockSpec((1, t, d), lambda i, j: (i, j, 0)),
                pl.BlockSpec((1, 6, d), lambda i, j: (i, 0, 0)),
                _resident((1, d)), _resident((d, d_ff)), _resident((d_ff, d))]
    args = [h, mod, g.reshape(1, d), w1, w2]
    if pending is not None:
        y, wo, bo = pending
        k = y.shape[-1]
        in_specs += [pl.BlockSpec((1, t, k), lambda i, j: (i, j, 0)), _resident((k, d)), _resident((1, d))]
        args += [y, wo, bo.reshape(1, d)]
    return pl.pallas_call(
        functools.partial(_mlp_kernel, ff_chunk=1024),
        out_shape=jax.ShapeDtypeStruct(h.shape, F32),
        grid=(b, s // t),
        in_specs=in_specs,
        out_specs=pl.BlockSpec((1, t, d), lambda i, j: (i, j, 0)),
        compiler_params=_params(2),
        name="mlp",
    )(*args)


def _fnet_chan_kernel(h_ref, mod_ref, g_ref, cs_ref, p_ref, q_ref, *, gc):
    n = _norm_mod(h_ref[0], g_ref[...], mod_ref, 0).astype(BF16)
    for gi in range(n.shape[1] // gc):
        r = _dot(n[:, gi * gc:(gi + 1) * gc], cs_ref[...])
        p_ref[0, :, gi * gc:(gi + 1) * gc] = r[:, :gc].astype(BF16)
        q_ref[0, :, gi * gc:(gi + 1) * gc] = r[:, gc:].astype(BF16)


def _fnet_pos_kernel(p_ref, q_ref, tw_ref, csm_ref, o_ref, y_ref, x_ref, *, radix, m, rc):
    tc = p_ref.shape[2]
    if radix == 1:
        y_ref[0, 0:m, :] = p_ref[0]
        y_ref[0, m:2 * m, :] = -q_ref[0]
        o_ref[0] = _dot(csm_ref[...], y_ref[0]).astype(BF16)
        return

    def chunk(i, carry):
        r0 = pl.multiple_of(i * rc, rc)
        a = [p_ref[0, pl.ds(q * m + r0, rc), :].astype(F32) for q in range(4)]
        b = [-q_ref[0, pl.ds(q * m + r0, rc), :].astype(F32) for q in range(4)]
        t0r, t0i = a[0] + a[2], b[0] + b[2]
        t1r, t1i = a[0] - a[2], b[0] - b[2]
        t2r, t2i = a[1] + a[3], b[1] + b[3]
        t3r, t3i = a[1] - a[3], b[1] - b[3]
        bs = [(t0r + t2r, t0i + t2i), (t1r + t3i, t1i - t3r),
              (t0r - t2r, t0i - t2i), (t1r - t3i, t1i + t3r)]
        for pp, (br, bi) in enumerate(bs):
            if pp == 0:
                yr, yi = br, bi
            else:
                c = jnp.tile(tw_ref[pp - 1, 0, pl.ds(r0, rc), :], (1, tc // LANES))
                s = jnp.tile(tw_ref[pp - 1, 1, pl.ds(r0, rc), :], (1, tc // LANES))
                yr = c * br + s * bi
                yi = c * bi - s * br
            y_ref[pp, pl.ds(r0, rc), :] = yr.astype(BF16)
            y_ref[pp, pl.ds(m + r0, rc), :] = yi.astype(BF16)
        return carry

    lax.fori_loop(0, m // rc, chunk, 0)
    for pp in range(radix):
        x = _dot(csm_ref[...], y_ref[pp])
        for cb in range(tc // LANES):
            x_ref[cb, pl.ds(pp, m, stride=radix), :] = x[:, cb * LANES:(cb + 1) * LANES]
    for cb in range(tc // LANES):
        o_ref[0, :, cb * LANES:(cb + 1) * LANES] = x_ref[cb].astype(BF16)


def _fnet_tables(s, gc):
    radix = 4 if s % 512 == 0 else 1
    m = s // radix
    ang_c = 2.0 * np.pi * np.outer(np.arange(gc), np.arange(gc)) / gc
    cs = np.concatenate([np.cos(ang_c), np.sin(ang_c)], axis=1)
    ang_m = 2.0 * np.pi * np.outer(np.arange(m), np.arange(m)) / m
    csm = np.concatenate([np.cos(ang_m), np.sin(ang_m)], axis=1) / math.sqrt(s * gc)
    if radix == 4:
        ang_t = 2.0 * np.pi * np.outer(np.arange(1, 4), np.arange(m)) / s
        tw = np.stack([np.cos(ang_t), np.sin(ang_t)], axis=1)
        tw = np.broadcast_to(tw[..., None], (3, 2, m, LANES))
    else:
        tw = np.zeros((1, 2, 8, LANES))
    return (radix, m, jnp.asarray(cs, BF16), jnp.asarray(csm, BF16), jnp.asarray(tw, F32))


def _fourier_mixer(h, mod, g):
    b, s, d = h.shape
    gc = d // FNET_GROUPS
    t = _row_tile(s)
    radix, m, cs, csm, tw = _fnet_tables(s, gc)
    p, q = pl.pallas_call(
        functools.partial(_fnet_chan_kernel, gc=gc),
        out_shape=(jax.ShapeDtypeStruct((b, s, d), BF16),) * 2,
        grid=(b, s // t),
        in_specs=[pl.BlockSpec((1, t, d), lambda i, j: (i, j, 0)),
                  pl.BlockSpec((1, 6, d), lambda i, j: (i, 0, 0)),
                  _resident((1, d)), _resident((gc, 2 * gc))],
        out_specs=(pl.BlockSpec((1, t, d), lambda i, j: (i, j, 0)),) * 2,
        compiler_params=_params(2),
        name="fnet_chan",
    )(h, mod, g.reshape(1, d), cs)
    tc = 256
    return pl.pallas_call(
        functools.partial(_fnet_pos_kernel, radix=radix, m=m, rc=min(m, 128)),
        out_shape=jax.ShapeDtypeStruct((b, s, d), BF16),
        grid=(b, d // tc),
        in_specs=[pl.BlockSpec((1, s, tc), lambda i, j: (i, 0, j)),
                  pl.BlockSpec((1, s, tc), lambda i, j: (i, 0, j)),
                  _resident(tw.shape), _resident((m, 2 * m))],
        out_specs=pl.BlockSpec((1, s, tc), lambda i, j: (i, 0, j)),
        scratch_shapes=[pltpu.VMEM((radix, 2 * m, tc), BF16),
                        pltpu.VMEM((tc // LANES, s if radix > 1 else SUBLANES, LANES), F32)],
        compiler_params=_params(2),
        name="fnet_pos",
    )(p, q, tw, csm)


def _qkv_kernel(h_ref, mod_ref, g_ref, w_ref, same_ref, hg_ref, cos_ref, sin_ref,
                q_ref, k_ref, v_ref, *, nq, nk, sub):
    t = h_ref.shape[1]

    def project(r0):
        n = _norm_mod(h_ref[0, r0:r0 + sub, :], g_ref[...], mod_ref, 0).astype(BF16)
        return _dot(n, w_ref[...])

    def epilogue(r0, acc):
        qk = acc[:, :nq + nk]
        sq = (qk * qk).astype(BF16)
        width = same_ref.shape[0]
        ssq = jnp.concatenate([_dot(sq[:, c:c + width], same_ref[...]) for c in range(0, nq + nk, width)], axis=1)
        qk = qk * lax.rsqrt(ssq * (1.0 / HEAD_DIM) + NORM_EPS) * hg_ref[...]
        cos = cos_ref[r0:r0 + sub, :]
        sin = sin_ref[r0:r0 + sub, :]
        for c in range((nq + nk) // LANES):
            x = qk[:, c * LANES:(c + 1) * LANES]
            y = (x * cos + pltpu.roll(x, LANES // 2, 1) * sin).astype(BF16)
            if c < nq // LANES:
                q_ref[0, r0:r0 + sub, c * LANES:(c + 1) * LANES] = y
            else:
                k_ref[0, r0:r0 + sub, c * LANES - nq:(c + 1) * LANES - nq] = y
        v_ref[0, r0:r0 + sub, :] = acc[:, nq + nk:].astype(BF16)

    _pipeline_rows(t, sub, project, epilogue)


def _block_lane_dims():
    lane = np.arange(LANES)
    e = lane % (2 * ROPE_PAIRS)
    first = np.where(e < ROPE_PAIRS, e, 2 * ROPE_PAIRS + e - ROPE_PAIRS)
    return first + ROPE_PAIRS * (lane // (LANES // 2)), (lane % (LANES // 2)) // (2 * ROPE_PAIRS)


def _qk_lane_layout(d):
    dim, slot = _block_lane_dims()
    n_qblocks = d // LANES
    cols, heads, dims = [], [], []
    for c in range(n_qblocks):
        head = 2 * c + slot
        cols.append(head * HEAD_DIM + dim)
        heads.append(head)
        dims.append(dim)
    for kvh in range(KV_HEADS):
        cols.append(d + kvh * HEAD_DIM + dim)
        heads.append(2 * n_qblocks + 2 * kvh + slot)
        dims.append(dim)
    is_q = np.arange((n_qblocks + KV_HEADS) * LANES) < d
    return np.concatenate(cols), np.concatenate(heads), np.concatenate(dims), is_q


def _rope_tables(s, with_rope):
    if not with_rope:
        return jnp.ones((s, LANES), F32), jnp.zeros((s, LANES), F32)
    rows = s // GRID_W
    row = jnp.repeat(jnp.arange(rows, dtype=F32), GRID_W)
    col = jnp.tile(jnp.arange(GRID_W, dtype=F32), rows)
    inv = ROPE_THETA ** (-jnp.arange(ROPE_PAIRS, dtype=F32) / ROPE_PAIRS)
    ang_r = row[:, None] * inv[None, :]
    ang_c = col[:, None] * inv[None, :]
    cos = jnp.concatenate([jnp.cos(ang_r)] * 2 + [jnp.cos(ang_c)] * 2, axis=1)
    sin = jnp.concatenate([jnp.sin(ang_r)] * 2 + [jnp.sin(ang_c)] * 2, axis=1)
    dim, _ = _block_lane_dims()
    sign = np.where(np.arange(LANES) < LANES // 2, -1.0, 1.0).astype(np.float32)
    return cos[:, dim], sin[:, dim] * sign


def _qkv(h, mod, g, w_perm, q_g, k_g, with_rope):
    b, s, d = h.shape
    nq = d
    nk = KV_HEADS * LANES
    t = _wide_row_tile(s)
    _, heads, dims, is_q = _qk_lane_layout(d)
    width = 2 * LANES
    same = (heads[:width, None] == heads[None, :width]).astype(np.float32)
    assert (nq + nk) % width == 0
    q_gain = q_g * (HEAD_DIM ** -0.5 * LOG2E)
    hg = jnp.where(is_q, q_gain[dims], k_g[dims]).reshape(1, nq + nk)
    cos, sin = _rope_tables(s, with_rope)
    return pl.pallas_call(
        functools.partial(_qkv_kernel, nq=nq, nk=nk, sub=min(t, 256)),
        out_shape=(jax.ShapeDtypeStruct((b, s, nq), BF16),
                   jax.ShapeDtypeStruct((b, s, nk), BF16),
                   jax.ShapeDtypeStruct((b, s, nk), BF16)),
        grid=(b, s // t),
        in_specs=[pl.BlockSpec((1, t, d), lambda i, j: (i, j, 0)),
                  pl.BlockSpec((1, 6, d), lambda i, j: (i, 0, 0)),
                  _resident((1, d)), _resident(w_perm.shape),
                  _resident((width, width)), _resident((1, nq + nk)),
                  pl.BlockSpec((t, LANES), lambda i, j: (j, 0)),
                  pl.BlockSpec((t, LANES), lambda i, j: (j, 0))],
        out_specs=(pl.BlockSpec((1, t, nq), lambda i, j: (i, j, 0)),
                   pl.BlockSpec((1, t, nk), lambda i, j: (i, j, 0)),
                   pl.BlockSpec((1, t, nk), lambda i, j: (i, j, 0))),
        compiler_params=_params(2),
        name="qkv",
    )(h, mod, g.reshape(1, d), w_perm, jnp.asarray(same, BF16), hg, cos, sin)


def _attn_kernel(sink_ref, q_ref, *refs, band, s_len, n_ctx):
    if band:
        k_ref, v_ref, kc_ref, vc_ref, o_ref = refs
    else:
        kc_ref, vc_ref, o_ref = refs
    j = pl.program_id(1)
    nq = q_ref.shape[1]
    n_band = 3 * ATTN_BLOCK if band else 0
    n_keys = n_band + n_ctx
    group = q_ref.shape[2] // (KV_HEADS * HEAD_DIM)
    pairs = group // 2
    lane = lax.broadcasted_iota(jnp.int32, (1, LANES), 1)
    low = lane < HEAD_DIM
    first = (lane % HEAD_DIM) < HEAD_DIM // 2

    if band:
        start = pl.multiple_of(jnp.clip((j - 1) * ATTN_BLOCK, 0, s_len - n_band), ATTN_BLOCK)
        qpos = j * ATTN_BLOCK + lax.broadcasted_iota(jnp.int32, (nq, n_band), 0)
        kpos = start + lax.broadcasted_iota(jnp.int32, (nq, n_band), 1)
        bias_band = jnp.where(jnp.abs(qpos - kpos) <= WINDOW, 0.0, NEG_BIG).astype(F32)

    k_bd, v_bd = [], []
    for kv in range(KV_HEADS):
        lanes_kv = slice(kv * LANES, (kv + 1) * LANES)
        k_all = kc_ref[0, :, lanes_kv]
        v_all = vc_ref[0, :, lanes_kv]
        if band:
            k_all = jnp.concatenate([k_ref[0, pl.ds(start, n_band), lanes_kv], k_all], axis=0)
            v_all = jnp.concatenate([v_ref[0, pl.ds(start, n_band), lanes_kv], v_all], axis=0)
        zero = jnp.zeros_like(k_all)
        k_bd.append(jnp.concatenate([jnp.where(first, k_all, zero), jnp.where(first, zero, k_all)], axis=0))
        v_bd.append(jnp.concatenate([jnp.where(low, v_all, zero), jnp.where(low, zero, v_all)], axis=0))

    def logits(kv, m):
        return _dot_nt(q_ref[0, :, (kv * pairs + m) * LANES:(kv * pairs + m + 1) * LANES], k_bd[kv])

    def softmax(kv, m, sc):
        probs, denoms = [], []
        for half in range(2):
            sh = sc[:, half * n_keys:(half + 1) * n_keys]
            if band:
                sh = jnp.concatenate([sh[:, :n_band] + bias_band, sh[:, n_band:]], axis=1)
            sink = sink_ref[kv * group + 2 * m + half] * LOG2E
            mx = jnp.maximum(jnp.max(sh, axis=1, keepdims=True), sink)
            e = jnp.exp2(sh - mx)
            denoms.append(jnp.sum(e, axis=1, keepdims=True) + jnp.exp2(sink - mx))
            probs.append(e.astype(BF16))
        return jnp.concatenate(probs, axis=1), jnp.where(low, denoms[0], denoms[1])

    def weighted_values(kv, m, p, denom):
        blk = slice((kv * pairs + m) * LANES, (kv * pairs + m + 1) * LANES)
        o_ref[0, :, blk] = (_dot(p, v_bd[kv]) / denom).astype(BF16)

    items = [(kv, m) for kv in range(KV_HEADS) for m in range(pairs)]
    sc = logits(*items[0])
    pending = None
    for idx, item in enumerate(items):
        nxt = logits(*items[idx + 1]) if idx + 1 < len(items) else None
        p, denom = softmax(*item, sc)
        if pending is not None:
            weighted_values(*pending)
        pending = (*item, p, denom)
        sc = nxt
    weighted_values(*pending)


def _attention(q, k, v, kc, vc, sink, band):
    b, s, d = q.shape
    n_ctx = kc.shape[1]
    nkv = kc.shape[2]
    tq = ATTN_BLOCK
    in_specs = [pl.BlockSpec(memory_space=pltpu.SMEM),
                pl.BlockSpec((1, tq, d), lambda i, j: (i, j, 0))]
    args = [sink.astype(F32), q]
    if band:
        in_specs += [pl.BlockSpec((1, s, nkv), lambda i, j: (i, 0, 0))] * 2
        args += [k, v]
    in_specs += [pl.BlockSpec((1, n_ctx, nkv), lambda i, j: (i, 0, 0))] * 2
    args += [kc, vc]
    return pl.pallas_call(
        functools.partial(_attn_kernel, band=band, s_len=s, n_ctx=n_ctx),
        out_shape=jax.ShapeDtypeStruct((b, s, d), BF16),
        grid=(b, s // tq),
        in_specs=in_specs,
        out_specs=pl.BlockSpec((1, tq, d), lambda i, j: (i, j, 0)),
        compiler_params=_params(2),
        name="attn_band" if band else "attn_ctx",
    )(*args)


def _permute_qkv_columns(w_qkv, d):
    cols, _, _, _ = _qk_lane_layout(d)
    v_off = d + KV_HEADS * HEAD_DIM
    v_cols = np.concatenate([v_off + hh * HEAD_DIM + np.arange(LANES) % HEAD_DIM for hh in range(KV_HEADS)])
    return w_qkv[:, np.concatenate([cols, v_cols])].astype(BF16)


def _glu_kernel(h_ref, mod_ref, g_ref, w_ref, b_ref, u_ref, *, sub):
    t, d = u_ref.shape[1], u_ref.shape[2]

    def project(r0):
        n = _norm_mod(h_ref[0, r0:r0 + sub, :], g_ref[...], mod_ref, 0).astype(BF16)
        return _dot(n, w_ref[...])

    def glu(r0, acc):
        a = acc + b_ref[...]
        u_ref[0, r0:r0 + sub, :] = a[:, :d] * jax.nn.sigmoid(a[:, d:])

    _pipeline_rows(t, sub, project, glu)


def _conv_kernel(u_ref, up_ref, un_ref, dw_ref, dwb_ref, lng_ref, lnb_ref, w_ref, b_ref, h_ref, mod_ref,
                 o_ref, ext_ref, y_ref, *, rows):
    j = pl.program_id(1)
    t, d = u_ref.shape[1], u_ref.shape[2]
    n_chunks = d // LANES
    for c in range(n_chunks):
        ls = slice(c * LANES, (c + 1) * LANES)
        ext_ref[c, 0:CONV_HALO, :] = jnp.where(j > 0, up_ref[0, :, ls], 0.0)
        ext_ref[c, CONV_HALO:CONV_HALO + t, :] = u_ref[0, :, ls]
        ext_ref[c, CONV_HALO + t:, :] = jnp.where(j < pl.num_programs(1) - 1, un_ref[0, :, ls], 0.0)
    shift = CONV_HALO - CONV_WIDTH // 2

    def lane_chunk(c, carry):
        for r0 in range(0, t, rows):
            acc = jnp.zeros((rows, LANES), F32)
            for tap in range(CONV_WIDTH):
                acc = acc + dw_ref[c, tap:tap + 1, :] * ext_ref[c, r0 + tap + shift:r0 + tap + shift + rows, :]
            y_ref[c, r0:r0 + rows, :] = acc
        return carry

    lax.fori_loop(0, n_chunks, lane_chunk, 0)
    y = jnp.concatenate([y_ref[c] for c in range(n_chunks)], axis=1)
    y = _layer_norm(y + dwb_ref[...], lng_ref[...], lnb_ref[...])
    y = (y * jax.nn.sigmoid(y)).astype(BF16)
    acc = _dot(y, w_ref[...]) + b_ref[...]
    o_ref[0] = h_ref[0] + mod_ref[0, 2:3, :] * acc


def _conv_module(h, mod, g, w_in, b_in, dw, dw_b, ln_g, ln_b, w_out, b_out):
    b, s, d = h.shape
    t = _row_tile(s)
    tg = _wide_row_tile(s)
    u = pl.pallas_call(
        functools.partial(_glu_kernel, sub=min(tg, 256)),
        out_shape=jax.ShapeDtypeStruct((b, s, d), F32),
        grid=(b, s // tg),
        in_specs=[pl.BlockSpec((1, tg, d), lambda i, j: (i, j, 0)),
                  pl.BlockSpec((1, 6, d), lambda i, j: (i, 0, 0)),
                  _resident((1, d)), _resident((d, 2 * d)), _resident((1, 2 * d))],
        out_specs=pl.BlockSpec((1, tg, d), lambda i, j: (i, j, 0)),
        compiler_params=_params(2),
        name="conv_glu",
    )(h, mod, g.reshape(1, d), w_in.astype(BF16), b_in.reshape(1, 2 * d))
    hb = t // CONV_HALO
    last = s // CONV_HALO - 1
    n_chunks = d // LANES
    taps_pad = -(-CONV_WIDTH // SUBLANES) * SUBLANES
    dw_pad = jnp.concatenate([dw, jnp.zeros((taps_pad - CONV_WIDTH, d), F32)], axis=0)
    dw_pad = dw_pad.reshape(taps_pad, n_chunks, LANES).transpose(1, 0, 2)
    return pl.pallas_call(
        functools.partial(_conv_kernel, rows=64),
        out_shape=jax.ShapeDtypeStruct(h.shape, F32),
        grid=(b, s // t),
        in_specs=[pl.BlockSpec((1, t, d), lambda i, j: (i, j, 0)),
                  pl.BlockSpec((1, CONV_HALO, d), lambda i, j: (i, jnp.maximum(j * hb - 1, 0), 0)),
                  pl.BlockSpec((1, CONV_HALO, d), lambda i, j: (i, jnp.minimum((j + 1) * hb, last), 0)),
                  _resident((n_chunks, taps_pad, LANES)), _resident((1, d)), _resident((1, d)), _resident((1, d)),
                  _resident((d, d)), _resident((1, d)),
                  pl.BlockSpec((1, t, d), lambda i, j: (i, j, 0)),
                  pl.BlockSpec((1, 6, d), lambda i, j: (i, 0, 0))],
        out_specs=pl.BlockSpec((1, t, d), lambda i, j: (i, j, 0)),
        scratch_shapes=[pltpu.VMEM((n_chunks, t + 2 * CONV_HALO, LANES), F32),
                        pltpu.VMEM((n_chunks, t, LANES), F32)],
        compiler_params=_params(2),
        name="conv_dw",
    )(u, u, u, dw_pad, dw_b.reshape(1, d), ln_g.reshape(1, d), ln_b.reshape(1, d),
      w_out.astype(BF16), b_out.reshape(1, d), h, mod)


def _gmlp_kernel(h_ref, mod_ref, g_ref, w_in_ref, b_in_ref, lng_ref, lnb_ref, ws_ref, bs_ref, w_out_ref,
                 o_ref, gate_ref, *, sub):
    t, d = h_ref.shape[1], h_ref.shape[2]
    gd = d // GMLP_GROUPS

    def project(r0):
        n = _norm_mod(h_ref[0, r0:r0 + sub, :], g_ref[...], mod_ref, 0).astype(BF16)
        return _dot(n, w_in_ref[...])

    def gate_and_project(r0, acc):
        z = jax.nn.gelu(acc + b_in_ref[...])
        u = z[:, :d]
        v = _layer_norm(z[:, d:], lng_ref[...], lnb_ref[...]).astype(BF16)
        for c in range(sub // GMLP_CHUNK):
            rs = slice(c * GMLP_CHUNK, (c + 1) * GMLP_CHUNK)
            for gi in range(GMLP_GROUPS):
                ls = slice(gi * gd, (gi + 1) * gd)
                sg = _dot(ws_ref[gi], v[rs, ls]) + jnp.tile(bs_ref[gi], (1, gd // LANES))
                gate_ref[r0 + c * GMLP_CHUNK:r0 + (c + 1) * GMLP_CHUNK, ls] = (u[rs, ls] * sg).astype(BF16)
        y = _dot(gate_ref[r0:r0 + sub, :], w_out_ref[...])
        o_ref[0, r0:r0 + sub, :] = h_ref[0, r0:r0 + sub, :] + mod_ref[0, 2:3, :] * y

    _pipeline_rows(t, sub, project, gate_and_project)


def _chunk_gmlp(h, mod, g, w_in, b_in, ln_g, ln_b, w_s, b_s, w_out):
    b, s, d = h.shape
    t = 1024 if s % 1024 == 0 else (256 if s % 256 == 0 else GMLP_CHUNK)
    bs = jnp.broadcast_to(b_s[:, :, None], (GMLP_GROUPS, GMLP_CHUNK, LANES)).astype(F32)
    return pl.pallas_call(
        functools.partial(_gmlp_kernel, sub=min(t, 256)),
        out_shape=jax.ShapeDtypeStruct(h.shape, F32),
        grid=(b, s // t),
        in_specs=[pl.BlockSpec((1, t, d), lambda i, j: (i, j, 0)),
                  pl.BlockSpec((1, 6, d), lambda i, j: (i, 0, 0)),
                  _resident((1, d)), _resident((d, 2 * d)), _resident((1, 2 * d)),
                  _resident((1, d)), _resident((1, d)),
                  _resident((GMLP_GROUPS, GMLP_CHUNK, GMLP_CHUNK)), _resident((GMLP_GROUPS, GMLP_CHUNK, LANES)),
                  _resident((d, d))],
        out_specs=pl.BlockSpec((1, t, d), lambda i, j: (i, j, 0)),
        scratch_shapes=[pltpu.VMEM((t, d), BF16)],
        compiler_params=_params(2),
        name="gmlp",
    )(h, mod, g.reshape(1, d), w_in.astype(BF16), b_in.reshape(1, 2 * d), ln_g.reshape(1, d),
      ln_b.reshape(1, d), w_s.astype(BF16), bs, w_out.astype(BF16))


def kernel(x, c, ctx, c_ctx, ada_w, ada_b, norm1_g, norm2_g, mlp_w1, mlp_w2, fnet_w, fnet_b, attn_w_qkv, attn_q_g, attn_k_g, attn_sink, attn_w_o, conv_w_in, conv_b_in, conv_dw, conv_dw_b, conv_ln_g, conv_ln_b, conv_w_out, conv_b_out, gmlp_w_in, gmlp_b_in, gmlp_ln_g, gmlp_ln_b, gmlp_w_s, gmlp_b_s, gmlp_w_out):
    bsz, _, d = x.shape
    depth = ada_w.shape[0]
    n_rows = -(-(bsz + 1) // 8) * 8
    c_rows = jnp.concatenate([c, c_ctx[None, :], jnp.zeros((n_rows - bsz - 1, d), F32)], axis=0)
    mods = _adaln(c_rows, ada_w, ada_b).reshape(depth, n_rows, 6, d)

    h_lat, h_ctx = x, ctx
    for i in range(depth):
        last = i == depth - 1
        kind, li = i % N_MIXERS, i // N_MIXERS
        mod_lat = mods[i, :bsz]
        mod_ctx = jnp.broadcast_to(mods[i, bsz], (bsz, 6, d))
        streams = [(h_lat, mod_lat)] + ([] if last else [(h_ctx, mod_ctx)])
        mixed = []
        if kind == 1:
            w_perm = _permute_qkv_columns(attn_w_qkv[li], d)
            w_o = attn_w_o[li].astype(BF16)
            zero_b = jnp.zeros((d,), F32)
            q, k, v = _qkv(h_lat, mod_lat, norm1_g[i], w_perm, attn_q_g[li], attn_k_g[li], True)
            qc, kc, vc = _qkv(h_ctx, mod_ctx, norm1_g[i], w_perm, attn_q_g[li], attn_k_g[li], False)
            mixed.append((h_lat, (_attention(q, k, v, kc, vc, attn_sink[li], True), w_o, zero_b)))
            if not last:
                mixed.append((h_ctx, (_attention(qc, None, None, kc, vc, attn_sink[li], False), w_o, zero_b)))
        else:
            for h, mod in streams:
                if kind == 0:
                    mixed.append((h, (_fourier_mixer(h, mod, norm1_g[i]), fnet_w[li].astype(BF16), fnet_b[li])))
                elif kind == 2:
                    mixed.append((_conv_module(h, mod, norm1_g[i], conv_w_in[li], conv_b_in[li], conv_dw[li],
                                               conv_dw_b[li], conv_ln_g[li], conv_ln_b[li], conv_w_out[li],
                                               conv_b_out[li]), None))
                else:
                    mixed.append((_chunk_gmlp(h, mod, norm1_g[i], gmlp_w_in[li], gmlp_b_in[li], gmlp_ln_g[li],
                                              gmlp_ln_b[li], gmlp_w_s[li], gmlp_b_s[li], gmlp_w_out[li]), None))
        w1 = mlp_w1[i].astype(BF16)
        w2 = mlp_w2[i].astype(BF16)
        h_lat = _mlp(mixed[0][0], mod_lat, norm2_g[i], w1, w2, mixed[0][1])
        if not last:
            h_ctx = _mlp(mixed[1][0], mod_ctx, norm2_g[i], w1, w2, mixed[1][1])
    return h_lat
```

```python
import functools
import math

import numpy as np
import jax
import jax.numpy as jnp
from jax import lax
from jax.experimental import pallas as pl
from jax.experimental.pallas import tpu as pltpu

F32 = jnp.float32
BF16 = jnp.bfloat16

NORM_EPS = 1e-6
GRID_W = 64
FNET_GROUPS = 4
HEAD_DIM = 64
KV_HEADS = 2
WINDOW = 128
ATTN_BLOCK = 128
ROPE_THETA = 10000.0
ROPE_PAIRS = HEAD_DIM // 4
CONV_WIDTH = 31
CONV_HALO = 16
GMLP_CHUNK = 128
GMLP_GROUPS = 4
N_MIXERS = 4
LANES = 128
SUBLANES = 8
VMEM_LIMIT = 56 * 1024 * 1024
NEG_BIG = -1e30
LOG2E = math.log2(math.e)


def _resident(shape):
    nd = len(shape)
    return pl.BlockSpec(shape, lambda *_: (0,) * nd, pipeline_mode=pl.Buffered(1))


def _params(n_grid):
    return pltpu.CompilerParams(dimension_semantics=("parallel",) * n_grid, vmem_limit_bytes=VMEM_LIMIT)


def _row_tile(s):
    return 512 if s % 512 == 0 else s


def _wide_row_tile(s):
    return 1024 if s % 1024 == 0 else _row_tile(s)


def _dot(a, b):
    return jnp.dot(a, b, preferred_element_type=F32)


def _dot_nt(a, b):
    return lax.dot_general(a, b, (((1,), (1,)), ((), ())), preferred_element_type=F32)


def _norm_mod(x, g, mod_ref, row):
    ms = jnp.mean(x * x, axis=-1, keepdims=True)
    y = x * lax.rsqrt(ms + NORM_EPS) * g
    return y * (1.0 + mod_ref[0, row + 1:row + 2, :]) + mod_ref[0, row:row + 1, :]


def _pipeline_rows(t, sub, project, finish):
    acc = project(0)
    for r0 in range(0, t, sub):
        nxt = project(r0 + sub) if r0 + sub < t else None
        finish(r0, acc)
        acc = nxt


def _layer_norm(x, g, b):
    mu = jnp.mean(x, axis=-1, keepdims=True)
    xc = x - mu
    var = jnp.mean(xc * xc, axis=-1, keepdims=True)
    return xc * lax.rsqrt(var + NORM_EPS) * g + b


def _adaln_kernel(c_ref, w_ref, b_ref, o_ref):
    c = c_ref[...]
    s = (c * jax.nn.sigmoid(c)).astype(BF16)
    o_ref[0] = _dot(s, w_ref[0].astype(BF16)) + b_ref[0]


def _adaln(c_rows, ada_w, ada_b):
    depth, d, n = ada_w.shape
    r = c_rows.shape[0]
    tn = 1024
    return pl.pallas_call(
        _adaln_kernel,
        out_shape=jax.ShapeDtypeStruct((depth, r, n), F32),
        grid=(depth, n // tn),
        in_specs=[pl.BlockSpec((r, d), lambda l, j: (0, 0)),
                  pl.BlockSpec((1, d, tn), lambda l, j: (l, 0, j)),
                  pl.BlockSpec((1, 1, tn), lambda l, j: (l, 0, j))],
        out_specs=pl.BlockSpec((1, r, tn), lambda l, j: (l, 0, j)),
        compiler_params=_params(2),
        name="adaln",
    )(c_rows, ada_w, ada_b.reshape(depth, 1, n))


def _mlp_kernel(h_ref, mod_ref, g_ref, w1_ref, w2_ref, *refs, ff_chunk):
    x = h_ref[0]
    if len(refs) == 4:
        y_ref, wo_ref, bo_ref, o_ref = refs
        x = x + mod_ref[0, 2:3, :] * (_dot(y_ref[0], wo_ref[...]) + bo_ref[...])
    else:
        o_ref, = refs
    n = _norm_mod(x, g_ref[...], mod_ref, 3).astype(BF16)
    d_ff = w1_ref.shape[2]
    acc = jnp.zeros(x.shape, F32)
    for c in range(d_ff // ff_chunk):
        a = _dot(n, w1_ref[0, :, c * ff_chunk:(c + 1) * ff_chunk])
        a = jnp.square(jnp.maximum(a, 0.0)).astype(BF16)
        acc = acc + _dot(a, w2_ref[0, c * ff_chunk:(c + 1) * ff_chunk, :])
    o_ref[0] = x + mod_ref[0, 5:6, :] * acc


def _mlp(h, mod, g, w1, w2, layer, pending=None):
    b, s, d = h.shape
    t = _row_tile(s)
    d_ff = w1.shape[2]
    in_specs = [pl.BlockSpec((1, t, d), lambda i, j: (i, j, 0)),
                pl.BlockSpec((1, 6, d), lambda i, j: (i, 0, 0)),
                _resident((1, d)),
                pl.BlockSpec((1, d, d_ff), lambda i, j: (layer, 0, 0), pipeline_mode=pl.Buffered(1)),
                pl.BlockSpec((1, d_ff, d), lambda i, j: (layer, 0, 0), pipeline_mode=pl.Buffered(1))]
    args = [h, mod, g.reshape(1, d), w1, w2]
    if pending is not None:
        y, wo, bo = pending
        k = y.shape[-1]
        in_specs += [pl.BlockSpec((1, t, k), lambda i, j: (i, j, 0)), _resident((k, d)), _resident((1, d))]
        args += [y, wo, bo.reshape(1, d)]
    return pl.pallas_call(
        functools.partial(_mlp_kernel, ff_chunk=1024),
        out_shape=jax.ShapeDtypeStruct(h.shape, F32),
        grid=(b, s // t),
        in_specs=in_specs,
        out_specs=pl.BlockSpec((1, t, d), lambda i, j: (i, j, 0)),
        compiler_params=_params(2),
        name="mlp",
    )(*args)


def _fnet_chan_kernel(h_ref, mod_ref, g_ref, cs_ref, p_ref, q_ref, *, gc):
    n = _norm_mod(h_ref[0], g_ref[...], mod_ref, 0).astype(BF16)
    for gi in range(n.shape[1] // gc):
        r = _dot(n[:, gi * gc:(gi + 1) * gc], cs_ref[...])
        p_ref[0, :, gi * gc:(gi + 1) * gc] = r[:, :gc].astype(BF16)
        q_ref[0, :, gi * gc:(gi + 1) * gc] = r[:, gc:].astype(BF16)


def _fnet_pos_kernel(p_ref, q_ref, tw_ref, csm_ref, o_ref, y_ref, x_ref, *, radix, m, rc):
    tc = p_ref.shape[2]
    if radix == 1:
        y_ref[0, 0:m, :] = p_ref[0]
        y_ref[0, m:2 * m, :] = -q_ref[0]
        o_ref[0] = _dot(csm_ref[...], y_ref[0]).astype(BF16)
        return

    def chunk(i, carry):
        r0 = pl.multiple_of(i * rc, rc)
        a = [p_ref[0, pl.ds(q * m + r0, rc), :].astype(F32) for q in range(4)]
        b = [-q_ref[0, pl.ds(q * m + r0, rc), :].astype(F32) for q in range(4)]
        t0r, t0i = a[0] + a[2], b[0] + b[2]
        t1r, t1i = a[0] - a[2], b[0] - b[2]
        t2r, t2i = a[1] + a[3], b[1] + b[3]
        t3r, t3i = a[1] - a[3], b[1] - b[3]
        bs = [(t0r + t2r, t0i + t2i), (t1r + t3i, t1i - t3r),
              (t0r - t2r, t0i - t2i), (t1r - t3i, t1i + t3r)]
        for pp, (br, bi) in enumerate(bs):
            if pp == 0:
                yr, yi = br, bi
            else:
                c = jnp.tile(tw_ref[pp - 1, 0, pl.ds(r0, rc), :], (1, tc // LANES))
                s = jnp.tile(tw_ref[pp - 1, 1, pl.ds(r0, rc), :], (1, tc // LANES))
                yr = c * br + s * bi
                yi = c * bi - s * br
            y_ref[pp, pl.ds(r0, rc), :] = yr.astype(BF16)
            y_ref[pp, pl.ds(m + r0, rc), :] = yi.astype(BF16)
        return carry

    lax.fori_loop(0, m // rc, chunk, 0)
    for pp in range(radix):
        x = _dot(csm_ref[...], y_ref[pp])
        for cb in range(tc // LANES):
            x_ref[cb, pl.ds(pp, m, stride=radix), :] = x[:, cb * LANES:(cb + 1) * LANES]
    for cb in range(tc // LANES):
        o_ref[0, :, cb * LANES:(cb + 1) * LANES] = x_ref[cb].astype(BF16)


def _fnet_tables(s, gc):
    radix = 4 if s % 512 == 0 else 1
    m = s // radix
    ang_c = 2.0 * np.pi * np.outer(np.arange(gc), np.arange(gc)) / gc
    cs = np.concatenate([np.cos(ang_c), np.sin(ang_c)], axis=1)
    ang_m = 2.0 * np.pi * np.outer(np.arange(m), np.arange(m)) / m
    csm = np.concatenate([np.cos(ang_m), np.sin(ang_m)], axis=1) / math.sqrt(s * gc)
    if radix == 4:
        ang_t = 2.0 * np.pi * np.outer(np.arange(1, 4), np.arange(m)) / s
        tw = np.stack([np.cos(ang_t), np.sin(ang_t)], axis=1)
        tw = np.broadcast_to(tw[..., None], (3, 2, m, LANES))
    else:
        tw = np.zeros((1, 2, 8, LANES))
    return (radix, m, jnp.asarray(cs, BF16), jnp.asarray(csm, BF16), jnp.asarray(tw, F32))


def _fourier_mixer(h, mod, g):
    b, s, d = h.shape
    gc = d // FNET_GROUPS
    t = _row_tile(s)
    radix, m, cs, csm, tw = _fnet_tables(s, gc)
    p, q = pl.pallas_call(
        functools.partial(_fnet_chan_kernel, gc=gc),
        out_shape=(jax.ShapeDtypeStruct((b, s, d), BF16),) * 2,
        grid=(b, s // t),
        in_specs=[pl.BlockSpec((1, t, d), lambda i, j: (i, j, 0)),
                  pl.BlockSpec((1, 6, d), lambda i, j: (i, 0, 0)),
                  _resident((1, d)), _resident((gc, 2 * gc))],
        out_specs=(pl.BlockSpec((1, t, d), lambda i, j: (i, j, 0)),) * 2,
        compiler_params=_params(2),
        name="fnet_chan",
    )(h, mod, g.reshape(1, d), cs)
    tc = 256
    return pl.pallas_call(
        functools.partial(_fnet_pos_kernel, radix=radix, m=m, rc=min(m, 128)),
        out_shape=jax.ShapeDtypeStruct((b, s, d), BF16),
        grid=(b, d // tc),
        in_specs=[pl.BlockSpec((1, s, tc), lambda i, j: (i, 0, j)),
                  pl.BlockSpec((1, s, tc), lambda i, j: (i, 0, j)),
                  _resident(tw.shape), _resident((m, 2 * m))],
        out_specs=pl.BlockSpec((1, s, tc), lambda i, j: (i, 0, j)),
        scratch_shapes=[pltpu.VMEM((radix, 2 * m, tc), BF16),
                        pltpu.VMEM((tc // LANES, s if radix > 1 else SUBLANES, LANES), F32)],
        compiler_params=_params(2),
        name="fnet_pos",
    )(p, q, tw, csm)


def _qkv_kernel(h_ref, mod_ref, g_ref, w_ref, same_ref, hg_ref, cos_ref, sin_ref,
                q_ref, k_ref, v_ref, *, nq, nk, sub):
    t = h_ref.shape[1]

    def project(r0):
        n = _norm_mod(h_ref[0, r0:r0 + sub, :], g_ref[...], mod_ref, 0).astype(BF16)
        return _dot(n, w_ref[...])

    def epilogue(r0, acc):
        qk = acc[:, :nq + nk]
        sq = (qk * qk).astype(BF16)
        width = same_ref.shape[0]
        ssq = jnp.concatenate([_dot(sq[:, c:c + width], same_ref[...]) for c in range(0, nq + nk, width)], axis=1)
        qk = qk * lax.rsqrt(ssq * (1.0 / HEAD_DIM) + NORM_EPS) * hg_ref[...]
        cos = cos_ref[r0:r0 + sub, :]
        sin = sin_ref[r0:r0 + sub, :]
        for c in range((nq + nk) // LANES):
            x = qk[:, c * LANES:(c + 1) * LANES]
            y = (x * cos + pltpu.roll(x, LANES // 2, 1) * sin).astype(BF16)
            if c < nq // LANES:
                q_ref[0, r0:r0 + sub, c * LANES:(c + 1) * LANES] = y
            else:
                k_ref[0, r0:r0 + sub, c * LANES - nq:(c + 1) * LANES - nq] = y
        v_ref[0, r0:r0 + sub, :] = acc[:, nq + nk:].astype(BF16)

    _pipeline_rows(t, sub, project, epilogue)


def _to_block_lanes(x, slots):
    lead = x.shape[:-1]
    x = x.reshape(*lead, -1, slots, 2, 2, ROPE_PAIRS)
    x = jnp.moveaxis(x, -2, -4)
    if slots == 1:
        x = jnp.broadcast_to(x, (*x.shape[:-3], 2, 2, ROPE_PAIRS))
    return x.reshape(*lead, -1)


def _rope_tables(s, with_rope):
    if not with_rope:
        return jnp.ones((s, LANES), F32), jnp.zeros((s, LANES), F32)
    rows = s // GRID_W
    row = jnp.repeat(jnp.arange(rows, dtype=F32), GRID_W)
    col = jnp.tile(jnp.arange(GRID_W, dtype=F32), rows)
    inv = ROPE_THETA ** (-jnp.arange(ROPE_PAIRS, dtype=F32) / ROPE_PAIRS)
    ang_r = row[:, None] * inv[None, :]
    ang_c = col[:, None] * inv[None, :]
    cos = jnp.concatenate([jnp.cos(ang_r), jnp.cos(ang_c)], axis=1)
    sin = jnp.concatenate([jnp.sin(ang_r), jnp.sin(ang_c)], axis=1)
    reps = LANES // (4 * ROPE_PAIRS)
    return jnp.tile(cos, (1, 2 * reps)), jnp.concatenate([jnp.tile(-sin, (1, reps)), jnp.tile(sin, (1, reps))], axis=1)


def _qkv(h, mod, g, w_perm, q_g, k_g, with_rope):
    b, s, d = h.shape
    nq = d
    nk = KV_HEADS * LANES
    t = _wide_row_tile(s)
    width = 2 * LANES
    lane = np.arange(width)
    head = 2 * (lane // LANES) + (lane % HEAD_DIM) // (HEAD_DIM // 2)
    same = (head[:, None] == head[None, :]).astype(np.float32)
    assert (nq + nk) % width == 0
    q_gain = _to_block_lanes(jnp.tile(q_g * (HEAD_DIM ** -0.5 * LOG2E), 2), 2)
    hg = jnp.concatenate([jnp.tile(q_gain, nq // LANES), jnp.tile(_to_block_lanes(k_g, 1), nk // LANES)])
    hg = hg.reshape(1, nq + nk)
    cos, sin = _rope_tables(s, with_rope)
    return pl.pallas_call(
        functools.partial(_qkv_kernel, nq=nq, nk=nk, sub=min(t, 256)),
        out_shape=(jax.ShapeDtypeStruct((b, s, nq), BF16),
                   jax.ShapeDtypeStruct((b, s, nk), BF16),
                   jax.ShapeDtypeStruct((b, s, nk), BF16)),
        grid=(b, s // t),
        in_specs=[pl.BlockSpec((1, t, d), lambda i, j: (i, j, 0)),
                  pl.BlockSpec((1, 6, d), lambda i, j: (i, 0, 0)),
                  _resident((1, d)), _resident(w_perm.shape),
                  _resident((width, width)), _resident((1, nq + nk)),
                  pl.BlockSpec((t, LANES), lambda i, j: (j, 0)),
                  pl.BlockSpec((t, LANES), lambda i, j: (j, 0))],
        out_specs=(pl.BlockSpec((1, t, nq), lambda i, j: (i, j, 0)),
                   pl.BlockSpec((1, t, nk), lambda i, j: (i, j, 0)),
                   pl.BlockSpec((1, t, nk), lambda i, j: (i, j, 0))),
        compiler_params=_params(2),
        name="qkv",
    )(h, mod, g.reshape(1, d), w_perm, jnp.asarray(same, BF16), hg, cos, sin)


def _attn_kernel(sink_ref, q_ref, *refs, band, s_len, n_ctx):
    if band:
        k_ref, v_ref, kc_ref, vc_ref, o_ref = refs
    else:
        kc_ref, vc_ref, o_ref = refs
    j = pl.program_id(1)
    nq = q_ref.shape[1]
    n_band = 3 * ATTN_BLOCK if band else 0
    n_keys = n_band + n_ctx
    group = q_ref.shape[2] // (KV_HEADS * HEAD_DIM)
    pairs = group // 2
    lane = lax.broadcasted_iota(jnp.int32, (1, LANES), 1)
    low = lane < HEAD_DIM
    first = (lane % HEAD_DIM) < HEAD_DIM // 2

    if band:
        start = pl.multiple_of(jnp.clip((j - 1) * ATTN_BLOCK, 0, s_len - n_band), ATTN_BLOCK)
        qpos = j * ATTN_BLOCK + lax.broadcasted_iota(jnp.int32, (nq, n_band), 0)
        kpos = start + lax.broadcasted_iota(jnp.int32, (nq, n_band), 1)
        bias_band = jnp.where(jnp.abs(qpos - kpos) <= WINDOW, 0.0, NEG_BIG).astype(F32)

    k_bd, v_bd = [], []
    for kv in range(KV_HEADS):
        lanes_kv = slice(kv * LANES, (kv + 1) * LANES)
        k_all = kc_ref[0, :, lanes_kv]
        v_all = vc_ref[0, :, lanes_kv]
        if band:
            k_all = jnp.concatenate([k_ref[0, pl.ds(start, n_band), lanes_kv], k_all], axis=0)
            v_all = jnp.concatenate([v_ref[0, pl.ds(start, n_band), lanes_kv], v_all], axis=0)
        zero = jnp.zeros_like(k_all)
        k_bd.append(jnp.concatenate([jnp.where(first, k_all, zero), jnp.where(first, zero, k_all)], axis=0))
        v_bd.append(jnp.concatenate([jnp.where(low, v_all, zero), jnp.where(low, zero, v_all)], axis=0))

    def logits(kv, m):
        return _dot_nt(q_ref[0, :, (kv * pairs + m) * LANES:(kv * pairs + m + 1) * LANES], k_bd[kv])

    def softmax(kv, m, sc):
        probs, denoms = [], []
        for half in range(2):
            sh = sc[:, half * n_keys:(half + 1) * n_keys]
            if band:
                sh = jnp.concatenate([sh[:, :n_band] + bias_band, sh[:, n_band:]], axis=1)
            sink = sink_ref[kv * group + 2 * m + half] * LOG2E
            mx = jnp.maximum(jnp.max(sh, axis=1, keepdims=True), sink)
            e = jnp.exp2(sh - mx)
            denoms.append(jnp.sum(e, axis=1, keepdims=True) + jnp.exp2(sink - mx))
            probs.append(e.astype(BF16))
        return jnp.concatenate(probs, axis=1), jnp.where(low, denoms[0], denoms[1])

    def weighted_values(kv, m, p, denom):
        blk = slice((kv * pairs + m) * LANES, (kv * pairs + m + 1) * LANES)
        o_ref[0, :, blk] = (_dot(p, v_bd[kv]) / denom).astype(BF16)

    items = [(kv, m) for kv in range(KV_HEADS) for m in range(pairs)]
    sc = logits(*items[0])
    pending = None
    for idx, item in enumerate(items):
        nxt = logits(*items[idx + 1]) if idx + 1 < len(items) else None
        p, denom = softmax(*item, sc)
        if pending is not None:
            weighted_values(*pending)
        pending = (*item, p, denom)
        sc = nxt
    weighted_values(*pending)


def _attention(q, k, v, kc, vc, sink, band):
    b, s, d = q.shape
    n_ctx = kc.shape[1]
    nkv = kc.shape[2]
    tq = ATTN_BLOCK
    in_specs = [pl.BlockSpec(memory_space=pltpu.SMEM),
                pl.BlockSpec((1, tq, d), lambda i, j: (i, j, 0))]
    args = [sink.astype(F32), q]
    if band:
        in_specs += [pl.BlockSpec((1, s, nkv), lambda i, j: (i, 0, 0))] * 2
        args += [k, v]
    in_specs += [pl.BlockSpec((1, n_ctx, nkv), lambda i, j: (i, 0, 0))] * 2
    args += [kc, vc]
    return pl.pallas_call(
        functools.partial(_attn_kernel, band=band, s_len=s, n_ctx=n_ctx),
        out_shape=jax.ShapeDtypeStruct((b, s, d), BF16),
        grid=(b, s // tq),
        in_specs=in_specs,
        out_specs=pl.BlockSpec((1, tq, d), lambda i, j: (i, j, 0)),
        compiler_params=_params(2),
        name="attn_band" if band else "attn_ctx",
    )(*args)


def _permute_qkv_columns(w_qkv, d):
    w = w_qkv.astype(BF16)
    nkv = KV_HEADS * HEAD_DIM
    wv = w[:, d + nkv:].reshape(-1, KV_HEADS, 1, HEAD_DIM)
    wv = jnp.broadcast_to(wv, (w.shape[0], KV_HEADS, LANES // HEAD_DIM, HEAD_DIM)).reshape(w.shape[0], -1)
    return jnp.concatenate([_to_block_lanes(w[:, :d], 2), _to_block_lanes(w[:, d:d + nkv], 1), wv], axis=1)


def _conv_kernel(h_ref, hp_ref, hn_ref, mod_ref, g_ref, w_in_ref, b_in_ref, dw_ref, dwb_ref, lng_ref,
                       lnb_ref, w_ref, b_ref, o_ref, n_ref, ext_ref, y_ref, *, rows):
    j = pl.program_id(1)
    t, d = h_ref.shape[1], h_ref.shape[2]
    n_chunks = d // LANES
    halo = CONV_HALO
    n_ref[0:halo, :] = _norm_mod(hp_ref[0], g_ref[...], mod_ref, 0).astype(BF16)
    n_ref[halo:halo + t, :] = _norm_mod(h_ref[0], g_ref[...], mod_ref, 0).astype(BF16)
    n_ref[halo + t:, :] = _norm_mod(hn_ref[0], g_ref[...], mod_ref, 0).astype(BF16)
    shift = halo - CONV_WIDTH // 2

    def glu(c):
        a = _dot(n_ref[...], w_in_ref[:, 2 * c * LANES:2 * (c + 1) * LANES]) + b_in_ref[:, 2 * c * LANES:2 * (c + 1) * LANES]
        u = a[:, :LANES] * jax.nn.sigmoid(a[:, LANES:])
        ext_ref[c, 0:halo, :] = jnp.where(j > 0, u[0:halo], 0.0)
        ext_ref[c, halo:halo + t, :] = u[halo:halo + t]
        ext_ref[c, halo + t:, :] = jnp.where(j < pl.num_programs(1) - 1, u[halo + t:], 0.0)

    def conv(c):
        for r0 in range(0, t, rows):
            acc = jnp.zeros((rows, LANES), F32)
            for tap in range(CONV_WIDTH):
                acc = acc + dw_ref[c, tap:tap + 1, :] * ext_ref[c, r0 + tap + shift:r0 + tap + shift + rows, :]
            y_ref[c, r0:r0 + rows, :] = acc

    glu(0)
    for c in range(n_chunks):
        if c + 1 < n_chunks:
            glu(c + 1)
        conv(c)
    y = jnp.concatenate([y_ref[c] for c in range(n_chunks)], axis=1)
    y = _layer_norm(y + dwb_ref[...], lng_ref[...], lnb_ref[...])
    y = (y * jax.nn.sigmoid(y)).astype(BF16)
    acc = _dot(y, w_ref[...]) + b_ref[...]
    o_ref[0] = h_ref[0] + mod_ref[0, 2:3, :] * acc


def _conv_module(h, mod, g, w_in, b_in, dw, dw_b, ln_g, ln_b, w_out, b_out):
    b, s, d = h.shape
    t = _row_tile(s)
    hb = t // CONV_HALO
    last = s // CONV_HALO - 1
    n_chunks = d // LANES
    taps_pad = -(-CONV_WIDTH // SUBLANES) * SUBLANES
    dw_pad = jnp.concatenate([dw, jnp.zeros((taps_pad - CONV_WIDTH, d), F32)], axis=0)
    dw_pad = dw_pad.reshape(taps_pad, n_chunks, LANES).transpose(1, 0, 2)

    def chunk_major(x):
        lead = x.shape[:-1]
        return x.reshape(*lead, 2, n_chunks, LANES).swapaxes(-2, -3).reshape(*lead, 2 * d)

    return pl.pallas_call(
        functools.partial(_conv_kernel, rows=64),
        out_shape=jax.ShapeDtypeStruct(h.shape, F32),
        grid=(b, s // t),
        in_specs=[pl.BlockSpec((1, t, d), lambda i, j: (i, j, 0)),
                  pl.BlockSpec((1, CONV_HALO, d), lambda i, j: (i, jnp.maximum(j * hb - 1, 0), 0)),
                  pl.BlockSpec((1, CONV_HALO, d), lambda i, j: (i, jnp.minimum((j + 1) * hb, last), 0)),
                  pl.BlockSpec((1, 6, d), lambda i, j: (i, 0, 0)),
                  _resident((1, d)), _resident((d, 2 * d)), _resident((1, 2 * d)),
                  _resident((n_chunks, taps_pad, LANES)), _resident((1, d)), _resident((1, d)), _resident((1, d)),
                  _resident((d, d)), _resident((1, d))],
        out_specs=pl.BlockSpec((1, t, d), lambda i, j: (i, j, 0)),
        scratch_shapes=[pltpu.VMEM((t + 2 * CONV_HALO, d), BF16),
                        pltpu.VMEM((n_chunks, t + 2 * CONV_HALO, LANES), F32),
                        pltpu.VMEM((n_chunks, t, LANES), F32)],
        compiler_params=_params(2),
        name="conv",
    )(h, h, h, mod, g.reshape(1, d), chunk_major(w_in.astype(BF16)), chunk_major(b_in.reshape(1, 2 * d)),
      dw_pad, dw_b.reshape(1, d), ln_g.reshape(1, d), ln_b.reshape(1, d), w_out.astype(BF16), b_out.reshape(1, d))


def _gmlp_kernel(h_ref, mod_ref, g_ref, w_in_ref, b_in_ref, lng_ref, lnb_ref, ws_ref, bs_ref, w_out_ref,
                 o_ref, gate_ref, *, sub):
    t, d = h_ref.shape[1], h_ref.shape[2]
    gd = d // GMLP_GROUPS

    def project(r0):
        n = _norm_mod(h_ref[0, r0:r0 + sub, :], g_ref[...], mod_ref, 0).astype(BF16)
        return _dot(n, w_in_ref[...])

    def gate(r0, acc):
        z = jax.nn.gelu(acc + b_in_ref[...])
        u = z[:, :d]
        v = _layer_norm(z[:, d:], lng_ref[...], lnb_ref[...]).astype(BF16)
        for c in range(sub // GMLP_CHUNK):
            rs = slice(c * GMLP_CHUNK, (c + 1) * GMLP_CHUNK)
            for gi in range(GMLP_GROUPS):
                ls = slice(gi * gd, (gi + 1) * gd)
                sg = _dot(ws_ref[gi], v[rs, ls]) + jnp.tile(bs_ref[gi], (1, gd // LANES))
                gate_ref[r0 + c * GMLP_CHUNK:r0 + (c + 1) * GMLP_CHUNK, ls] = (u[rs, ls] * sg).astype(BF16)

    def project_out(r0):
        y = _dot(gate_ref[r0:r0 + sub, :], w_out_ref[...])
        o_ref[0, r0:r0 + sub, :] = h_ref[0, r0:r0 + sub, :] + mod_ref[0, 2:3, :] * y

    acc = project(0)
    for r0 in range(0, t, sub):
        nxt = project(r0 + sub) if r0 + sub < t else None
        gate(r0, acc)
        if r0 > 0:
            project_out(r0 - sub)
        acc = nxt
    project_out(t - sub)


def _chunk_gmlp(h, mod, g, w_in, b_in, ln_g, ln_b, w_s, b_s, w_out):
    b, s, d = h.shape
    t = 1024 if s % 1024 == 0 else (256 if s % 256 == 0 else GMLP_CHUNK)
    bs = jnp.broadcast_to(b_s[:, :, None], (GMLP_GROUPS, GMLP_CHUNK, LANES)).astype(F32)
    return pl.pallas_call(
        functools.partial(_gmlp_kernel, sub=min(t, 256)),
        out_shape=jax.ShapeDtypeStruct(h.shape, F32),
        grid=(b, s // t),
        in_specs=[pl.BlockSpec((1, t, d), lambda i, j: (i, j, 0)),
                  pl.BlockSpec((1, 6, d), lambda i, j: (i, 0, 0)),
                  _resident((1, d)), _resident((d, 2 * d)), _resident((1, 2 * d)),
                  _resident((1, d)), _resident((1, d)),
                  _resident((GMLP_GROUPS, GMLP_CHUNK, GMLP_CHUNK)), _resident((GMLP_GROUPS, GMLP_CHUNK, LANES)),
                  _resident((d, d))],
        out_specs=pl.BlockSpec((1, t, d), lambda i, j: (i, j, 0)),
        scratch_shapes=[pltpu.VMEM((t, d), BF16)],
        compiler_params=_params(2),
        name="gmlp",
    )(h, mod, g.reshape(1, d), w_in.astype(BF16), b_in.reshape(1, 2 * d), ln_g.reshape(1, d),
      ln_b.reshape(1, d), w_s.astype(BF16), bs, w_out.astype(BF16))


def kernel(x, c, ctx, c_ctx, ada_w, ada_b, norm1_g, norm2_g, mlp_w1, mlp_w2, fnet_w, fnet_b, attn_w_qkv, attn_q_g, attn_k_g, attn_sink, attn_w_o, conv_w_in, conv_b_in, conv_dw, conv_dw_b, conv_ln_g, conv_ln_b, conv_w_out, conv_b_out, gmlp_w_in, gmlp_b_in, gmlp_ln_g, gmlp_ln_b, gmlp_w_s, gmlp_b_s, gmlp_w_out):
    bsz, _, d = x.shape
    depth = ada_w.shape[0]
    n_rows = -(-(bsz + 1) // 8) * 8
    c_rows = jnp.concatenate([c, c_ctx[None, :], jnp.zeros((n_rows - bsz - 1, d), F32)], axis=0)
    mods = _adaln(c_rows, ada_w, ada_b).reshape(depth, n_rows, 6, d)

    w1_all = mlp_w1.astype(BF16)
    w2_all = mlp_w2.astype(BF16)
    h_lat, h_ctx = x, ctx
    for i in range(depth):
        last = i == depth - 1
        kind, li = i % N_MIXERS, i // N_MIXERS
        mod_lat = mods[i, :bsz]
        mod_ctx = jnp.broadcast_to(mods[i, bsz], (bsz, 6, d))
        streams = [(h_lat, mod_lat)] + ([] if last else [(h_ctx, mod_ctx)])
        mixed = []
        if kind == 1:
            w_perm = _permute_qkv_columns(attn_w_qkv[li], d)
            w_o = attn_w_o[li].astype(BF16)
            zero_b = jnp.zeros((d,), F32)
            q, k, v = _qkv(h_lat, mod_lat, norm1_g[i], w_perm, attn_q_g[li], attn_k_g[li], True)
            qc, kc, vc = _qkv(h_ctx, mod_ctx, norm1_g[i], w_perm, attn_q_g[li], attn_k_g[li], False)
            mixed.append((h_lat, (_attention(q, k, v, kc, vc, attn_sink[li], True), w_o, zero_b)))
            if not last:
                mixed.append((h_ctx, (_attention(qc, None, None, kc, vc, attn_sink[li], False), w_o, zero_b)))
        else:
            for h, mod in streams:
                if kind == 0:
                    mixed.append((h, (_fourier_mixer(h, mod, norm1_g[i]), fnet_w[li].astype(BF16), fnet_b[li])))
                elif kind == 2:
                    mixed.append((_conv_module(h, mod, norm1_g[i], conv_w_in[li], conv_b_in[li], conv_dw[li],
                                               conv_dw_b[li], conv_ln_g[li], conv_ln_b[li], conv_w_out[li],
                                               conv_b_out[li]), None))
                else:
                    mixed.append((_chunk_gmlp(h, mod, norm1_g[i], gmlp_w_in[li], gmlp_b_in[li], gmlp_ln_g[li],
                                              gmlp_ln_b[li], gmlp_w_s[li], gmlp_b_s[li], gmlp_w_out[li]), None))
        h_lat = _mlp(mixed[0][0], mod_lat, norm2_g[i], w1_all, w2_all, i, mixed[0][1])
        if not last:
            h_ctx = _mlp(mixed[1][0], mod_ctx, norm2_g[i], w1_all, w2_all, i, mixed[1][1])
    return h_lat
```

```python
import functools
import math

import numpy as np
import jax
import jax.numpy as jnp
from jax import lax
from jax.experimental import pallas as pl
from jax.experimental.pallas import tpu as pltpu

F32 = jnp.float32
BF16 = jnp.bfloat16

NORM_EPS = 1e-6
GRID_W = 64
FNET_GROUPS = 4
HEAD_DIM = 64
KV_HEADS = 2
WINDOW = 128
ATTN_BLOCK = 128
ROPE_THETA = 10000.0
ROPE_PAIRS = HEAD_DIM // 4
CONV_WIDTH = 31
CONV_HALO = 16
GMLP_CHUNK = 128
GMLP_GROUPS = 4
N_MIXERS = 4
LANES = 128
SUBLANES = 8
VMEM_LIMIT = 56 * 1024 * 1024
NEG_BIG = -1e30
LOG2E = math.log2(math.e)


def _resident(shape):
    nd = len(shape)
    return pl.BlockSpec(shape, lambda *_: (0,) * nd, pipeline_mode=pl.Buffered(1))


def _params(n_grid):
    return pltpu.CompilerParams(dimension_semantics=("parallel",) * n_grid, vmem_limit_bytes=VMEM_LIMIT)


def _row_tile(s):
    return 512 if s % 512 == 0 else s


def _wide_row_tile(s):
    return 1024 if s % 1024 == 0 else _row_tile(s)


def _dot(a, b):
    return jnp.dot(a, b, preferred_element_type=F32)


def _dot_nt(a, b):
    return lax.dot_general(a, b, (((1,), (1,)), ((), ())), preferred_element_type=F32)


def _norm_mod(x, g, mod_ref, row):
    ms = jnp.mean(x * x, axis=-1, keepdims=True)
    gain = g * (1.0 + mod_ref[0, row + 1:row + 2, :])
    return x * lax.rsqrt(ms + NORM_EPS) * gain + mod_ref[0, row:row + 1, :]


def _pipeline_rows(t, sub, project, finish):
    acc = project(0)
    for r0 in range(0, t, sub):
        nxt = project(r0 + sub) if r0 + sub < t else None
        finish(r0, acc)
        acc = nxt


def _layer_norm(x, g, b):
    mu = jnp.mean(x, axis=-1, keepdims=True)
    xc = x - mu
    var = jnp.mean(xc * xc, axis=-1, keepdims=True)
    return xc * lax.rsqrt(var + NORM_EPS) * g + b


def _adaln_kernel(c_ref, w_ref, b_ref, o_ref):
    c = c_ref[...]
    s = (c * jax.nn.sigmoid(c)).astype(BF16)
    o_ref[0] = _dot(s, w_ref[0].astype(BF16)) + b_ref[0]


def _adaln(c_rows, ada_w, ada_b):
    depth, d, n = ada_w.shape
    r = c_rows.shape[0]
    tn = 1024
    return pl.pallas_call(
        _adaln_kernel,
        out_shape=jax.ShapeDtypeStruct((depth, r, n), F32),
        grid=(depth, n // tn),
        in_specs=[pl.BlockSpec((r, d), lambda l, j: (0, 0)),
                  pl.BlockSpec((1, d, tn), lambda l, j: (l, 0, j)),
                  pl.BlockSpec((1, 1, tn), lambda l, j: (l, 0, j))],
        out_specs=pl.BlockSpec((1, r, tn), lambda l, j: (l, 0, j)),
        compiler_params=_params(2),
        name="adaln",
    )(c_rows, ada_w, ada_b.reshape(depth, 1, n))


def _mlp_kernel(h_ref, mod_ref, g_ref, w1_ref, w2_ref, *refs, ff_chunk):
    x = h_ref[0]
    if len(refs) == 4:
        y_ref, wo_ref, bo_ref, o_ref = refs
        x = x + mod_ref[0, 2:3, :] * (_dot(y_ref[0], wo_ref[...]) + bo_ref[...])
    else:
        o_ref, = refs
    n = _norm_mod(x, g_ref[...], mod_ref, 3).astype(BF16)
    d_ff = w1_ref.shape[2]
    acc = jnp.zeros(x.shape, F32)
    for c in range(d_ff // ff_chunk):
        a = _dot(n, w1_ref[0, :, c * ff_chunk:(c + 1) * ff_chunk])
        a = jnp.square(jnp.maximum(a, 0.0)).astype(BF16)
        acc = acc + _dot(a, w2_ref[0, c * ff_chunk:(c + 1) * ff_chunk, :])
    o_ref[0] = x + mod_ref[0, 5:6, :] * acc


def _mlp(h, mod, g, w1, w2, layer, pending=None):
    b, s, d = h.shape
    t = _wide_row_tile(s)
    d_ff = w1.shape[2]
    in_specs = [pl.BlockSpec((1, t, d), lambda i, j: (i, j, 0)),
                pl.BlockSpec((1, 6, d), lambda i, j: (i, 0, 0)),
                _resident((1, d)),
                pl.BlockSpec((1, d, d_ff), lambda i, j: (layer, 0, 0), pipeline_mode=pl.Buffered(1)),
                pl.BlockSpec((1, d_ff, d), lambda i, j: (layer, 0, 0), pipeline_mode=pl.Buffered(1))]
    args = [h, mod, g.reshape(1, d), w1, w2]
    if pending is not None:
        y, wo, bo = pending
        k = y.shape[-1]
        in_specs += [pl.BlockSpec((1, t, k), lambda i, j: (i, j, 0)), _resident((k, d)), _resident((1, d))]
        args += [y, wo, bo.reshape(1, d)]
    return pl.pallas_call(
        functools.partial(_mlp_kernel, ff_chunk=1024),
        out_shape=jax.ShapeDtypeStruct(h.shape, F32),
        grid=(b, s // t),
        in_specs=in_specs,
        out_specs=pl.BlockSpec((1, t, d), lambda i, j: (i, j, 0)),
        compiler_params=_params(2),
        name="mlp",
    )(*args)


def _scaled_sum(ca, a, cb, b):
    mag = abs(ca)
    if ca > 0 and cb > 0:
        return mag * (a + b)
    if ca > 0:
        return mag * (a - b)
    if cb > 0:
        return mag * (b - a)
    return (-mag) * (a + b)


def _small_dft(z):
    r = len(z)
    if r == 1:
        return z
    even = _small_dft(z[0::2])
    odd = _small_dft(z[1::2])
    out = [None] * r
    for k in range(r // 2):
        (er, ei), (a, b) = even[k], odd[k]
        if k == 0:
            tr, ti = a, b
        elif 4 * k == r:
            out[k] = (er + b, ei - a)
            out[k + r // 2] = (er - b, ei + a)
            continue
        else:
            c, s = math.cos(2 * math.pi * k / r), math.sin(2 * math.pi * k / r)
            if abs(abs(c) - abs(s)) < 1e-12:
                tr, ti = _scaled_sum(c, a, s, b), _scaled_sum(c, b, -s, a)
            else:
                tr, ti = c * a + s * b, c * b - s * a
        out[k] = (er + tr, ei + ti)
        out[k + r // 2] = (er - tr, ei - ti)
    return out


def _fnet_chan_kernel(*refs, gc, radix):
    h_refs = refs[:radix]
    mod_ref, g_ref, cs_ref, tw_ref, yr_ref, yi_ref, n_ref = refs[radix:]
    d = h_refs[0].shape[2]
    for q in range(radix):
        n_ref[q] = _norm_mod(h_refs[q][0], g_ref[...], mod_ref, 0).astype(BF16)
    for gi in range(d // gc):
        ls = slice(gi * gc, (gi + 1) * gc)
        z = []
        for q in range(radix):
            r = _dot(n_ref[q, :, ls], cs_ref[...])
            z.append((r[:, :gc], r[:, gc:]))
        for p, (br, bi) in enumerate(_small_dft(z)):
            if p > 0:
                c = jnp.tile(tw_ref[p - 1, 0], (1, gc // LANES))
                s = jnp.tile(tw_ref[p - 1, 1], (1, gc // LANES))
                br, bi = c * br + s * bi, c * bi - s * br
            yr_ref[0, p, :, ls] = br.astype(BF16)
            yi_ref[0, p, :, ls] = bi.astype(BF16)


def _fnet_pos_kernel(yr_ref, yi_ref, csm_ref, o_ref, x_ref, *, radix, m):
    tc = yr_ref.shape[3]
    for p in range(radix):
        x = _dot(csm_ref[...], jnp.concatenate([yr_ref[0, p], yi_ref[0, p]], axis=0))
        if radix == 1:
            o_ref[0] = x.astype(BF16)
        else:
            for cb in range(tc // LANES):
                x_ref[cb, pl.ds(p, m, stride=radix), :] = x[:, cb * LANES:(cb + 1) * LANES]
    if radix > 1:
        for cb in range(tc // LANES):
            o_ref[0, :, cb * LANES:(cb + 1) * LANES] = x_ref[cb].astype(BF16)


def _fnet_tables(s, gc):
    radix = 8 if s % 1024 == 0 else 1
    m = s // radix
    ang_c = 2.0 * np.pi * np.outer(np.arange(gc), np.arange(gc)) / gc
    cs = np.concatenate([np.cos(ang_c), -np.sin(ang_c)], axis=1)
    ang_m = 2.0 * np.pi * np.outer(np.arange(m), np.arange(m)) / m
    csm = np.concatenate([np.cos(ang_m), np.sin(ang_m)], axis=1) / math.sqrt(s * gc)
    if radix > 1:
        ang_t = 2.0 * np.pi * np.outer(np.arange(1, radix), np.arange(m)) / s
        tw = np.stack([np.cos(ang_t), np.sin(ang_t)], axis=1)
        tw = np.broadcast_to(tw[..., None], (radix - 1, 2, m, LANES))
    else:
        tw = np.zeros((1, 2, m, LANES))
    return (radix, m, jnp.asarray(cs, BF16), jnp.asarray(csm, BF16), jnp.asarray(tw, F32))


def _fourier_mixer(h, mod, g):
    b, s, d = h.shape
    gc = d // FNET_GROUPS
    radix, m, cs, csm, tw = _fnet_tables(s, gc)
    tm = min(m, 128)
    steps = m // tm
    h_specs = [pl.BlockSpec((1, tm, d), lambda i, j, q=q: (i, q * steps + j, 0)) for q in range(radix)]
    y_spec = pl.BlockSpec((1, radix, tm, d), lambda i, j: (i, 0, j, 0))
    yr, yi = pl.pallas_call(
        functools.partial(_fnet_chan_kernel, gc=gc, radix=radix),
        out_shape=(jax.ShapeDtypeStruct((b, radix, m, d), BF16),) * 2,
        grid=(b, steps),
        in_specs=h_specs + [pl.BlockSpec((1, 6, d), lambda i, j: (i, 0, 0)),
                            _resident((1, d)), _resident((gc, 2 * gc)),
                            pl.BlockSpec((tw.shape[0], 2, tm, LANES), lambda i, j: (0, 0, j, 0))],
        out_specs=(y_spec, y_spec),
        scratch_shapes=[pltpu.VMEM((radix, tm, d), BF16)],
        compiler_params=_params(2),
        name="fnet_chan",
    )(*([h] * radix), mod, g.reshape(1, d), cs, tw)
    tc = 256
    return pl.pallas_call(
        functools.partial(_fnet_pos_kernel, radix=radix, m=m),
        out_shape=jax.ShapeDtypeStruct((b, s, d), BF16),
        grid=(b, d // tc),
        in_specs=[pl.BlockSpec((1, radix, m, tc), lambda i, j: (i, 0, 0, j)),
                  pl.BlockSpec((1, radix, m, tc), lambda i, j: (i, 0, 0, j)),
                  _resident((m, 2 * m))],
        out_specs=pl.BlockSpec((1, s, tc), lambda i, j: (i, 0, j)),
        scratch_shapes=[pltpu.VMEM((tc // LANES, s if radix > 1 else SUBLANES, LANES), F32)],
        compiler_params=_params(2),
        name="fnet_pos",
    )(yr, yi, csm)


def _qkv_kernel(h_ref, mod_ref, g_ref, w_ref, same_ref, hg_ref, cos_ref, sin_ref,
                q_ref, k_ref, v_ref, *, nq, nk, sub):
    t = h_ref.shape[1]

    def project(r0):
        n = _norm_mod(h_ref[0, r0:r0 + sub, :], g_ref[...], mod_ref, 0).astype(BF16)
        return _dot(n, w_ref[...])

    def epilogue(r0, acc):
        qk = acc[:, :nq + nk]
        sq = (qk * qk).astype(BF16)
        width = same_ref.shape[0]
        ssq = jnp.concatenate([_dot(sq[:, c:c + width], same_ref[...]) for c in range(0, nq + nk, width)], axis=1)
        qk = qk * lax.rsqrt(ssq * (1.0 / HEAD_DIM) + NORM_EPS) * hg_ref[...]
        cos = cos_ref[r0:r0 + sub, :]
        sin = sin_ref[r0:r0 + sub, :]
        for c in range((nq + nk) // LANES):
            x = qk[:, c * LANES:(c + 1) * LANES]
            y = (x * cos + pltpu.roll(x, LANES // 2, 1) * sin).astype(BF16)
            if c < nq // LANES:
                q_ref[0, r0:r0 + sub, c * LANES:(c + 1) * LANES] = y
            else:
                k_ref[0, r0:r0 + sub, c * LANES - nq:(c + 1) * LANES - nq] = y
        v_ref[0, r0:r0 + sub, :] = acc[:, nq + nk:].astype(BF16)

    _pipeline_rows(t, sub, project, epilogue)


def _to_block_lanes(x, slots):
    lead = x.shape[:-1]
    x = x.reshape(*lead, -1, slots, 2, 2, ROPE_PAIRS)
    x = jnp.moveaxis(x, -2, -4)
    if slots == 1:
        x = jnp.broadcast_to(x, (*x.shape[:-3], 2, 2, ROPE_PAIRS))
    return x.reshape(*lead, -1)


def _rope_tables(s, with_rope):
    if not with_rope:
        return jnp.ones((s, LANES), F32), jnp.zeros((s, LANES), F32)
    rows = s // GRID_W
    row = jnp.repeat(jnp.arange(rows, dtype=F32), GRID_W)
    col = jnp.tile(jnp.arange(GRID_W, dtype=F32), rows)
    inv = ROPE_THETA ** (-jnp.arange(ROPE_PAIRS, dtype=F32) / ROPE_PAIRS)
    ang_r = row[:, None] * inv[None, :]
    ang_c = col[:, None] * inv[None, :]
    cos = jnp.concatenate([jnp.cos(ang_r), jnp.cos(ang_c)], axis=1)
    sin = jnp.concatenate([jnp.sin(ang_r), jnp.sin(ang_c)], axis=1)
    reps = LANES // (4 * ROPE_PAIRS)
    return jnp.tile(cos, (1, 2 * reps)), jnp.concatenate([jnp.tile(-sin, (1, reps)), jnp.tile(sin, (1, reps))], axis=1)


def _qkv(h, mod, g, w_perm, q_g, k_g, with_rope):
    b, s, d = h.shape
    nq = d
    nk = KV_HEADS * LANES
    t = _wide_row_tile(s)
    width = 2 * LANES
    lane = np.arange(width)
    head = 2 * (lane // LANES) + (lane % HEAD_DIM) // (HEAD_DIM // 2)
    same = (head[:, None] == head[None, :]).astype(np.float32)
    assert (nq + nk) % width == 0
    q_gain = _to_block_lanes(jnp.tile(q_g * (HEAD_DIM ** -0.5 * LOG2E), 2), 2)
    hg = jnp.concatenate([jnp.tile(q_gain, nq // LANES), jnp.tile(_to_block_lanes(k_g, 1), nk // LANES)])
    hg = hg.reshape(1, nq + nk)
    cos, sin = _rope_tables(s, with_rope)
    return pl.pallas_call(
        functools.partial(_qkv_kernel, nq=nq, nk=nk, sub=min(t, 256)),
        out_shape=(jax.ShapeDtypeStruct((b, s, nq), BF16),
                   jax.ShapeDtypeStruct((b, s, nk), BF16),
                   jax.ShapeDtypeStruct((b, s, nk), BF16)),
        grid=(b, s // t),
        in_specs=[pl.BlockSpec((1, t, d), lambda i, j: (i, j, 0)),
                  pl.BlockSpec((1, 6, d), lambda i, j: (i, 0, 0)),
                  _resident((1, d)), _resident(w_perm.shape),
                  _resident((width, width)), _resident((1, nq + nk)),
                  pl.BlockSpec((t, LANES), lambda i, j: (j, 0)),
                  pl.BlockSpec((t, LANES), lambda i, j: (j, 0))],
        out_specs=(pl.BlockSpec((1, t, nq), lambda i, j: (i, j, 0)),
                   pl.BlockSpec((1, t, nk), lambda i, j: (i, j, 0)),
                   pl.BlockSpec((1, t, nk), lambda i, j: (i, j, 0))),
        compiler_params=_params(2),
        name="qkv",
    )(h, mod, g.reshape(1, d), w_perm, jnp.asarray(same, BF16), hg, cos, sin)


def _attn_kernel(sink_ref, q_ref, *refs, band, s_len, n_ctx):
    if band:
        k_ref, v_ref, kc_ref, vc_ref, o_ref = refs
    else:
        kc_ref, vc_ref, o_ref = refs
    j = pl.program_id(1)
    nq = q_ref.shape[1]
    n_band = 3 * ATTN_BLOCK if band else 0
    n_keys = n_band + n_ctx
    group = q_ref.shape[2] // (KV_HEADS * HEAD_DIM)
    pairs = group // 2
    lane = lax.broadcasted_iota(jnp.int32, (1, LANES), 1)
    low = lane < HEAD_DIM
    first = (lane % HEAD_DIM) < HEAD_DIM // 2

    if band:
        start = pl.multiple_of(jnp.clip((j - 1) * ATTN_BLOCK, 0, s_len - n_band), ATTN_BLOCK)
        qpos = j * ATTN_BLOCK + lax.broadcasted_iota(jnp.int32, (nq, n_band), 0)
        kpos = start + lax.broadcasted_iota(jnp.int32, (nq, n_band), 1)
        bias_band = jnp.where(jnp.abs(qpos - kpos) <= WINDOW, 0.0, NEG_BIG).astype(F32)

    k_bd, v_bd = [], []
    for kv in range(KV_HEADS):
        lanes_kv = slice(kv * LANES, (kv + 1) * LANES)
        k_all = kc_ref[0, :, lanes_kv]
        v_all = vc_ref[0, :, lanes_kv]
        if band:
            k_all = jnp.concatenate([k_ref[0, pl.ds(start, n_band), lanes_kv], k_all], axis=0)
            v_all = jnp.concatenate([v_ref[0, pl.ds(start, n_band), lanes_kv], v_all], axis=0)
        zero = jnp.zeros_like(k_all)
        k_bd.append(jnp.concatenate([jnp.where(first, k_all, zero), jnp.where(first, zero, k_all)], axis=0))
        v_bd.append(jnp.concatenate([jnp.where(low, v_all, zero), jnp.where(low, zero, v_all)], axis=0))

    def logits(kv, m):
        return _dot_nt(q_ref[0, :, (kv * pairs + m) * LANES:(kv * pairs + m + 1) * LANES], k_bd[kv])

    def softmax(kv, m, sc):
        probs, denoms = [], []
        for half in range(2):
            sh = sc[:, half * n_keys:(half + 1) * n_keys]
            if band:
                sh = jnp.concatenate([sh[:, :n_band] + bias_band, sh[:, n_band:]], axis=1)
            sink = sink_ref[kv * group + 2 * m + half] * LOG2E
            mx = jnp.maximum(jnp.max(sh, axis=1, keepdims=True), sink)
            e = jnp.exp2(sh - mx)
            denoms.append(jnp.sum(e, axis=1, keepdims=True) + jnp.exp2(sink - mx))
            probs.append(e.astype(BF16))
        return jnp.concatenate(probs, axis=1), jnp.where(low, denoms[0], denoms[1])

    def weighted_values(kv, m, p, denom):
        blk = slice((kv * pairs + m) * LANES, (kv * pairs + m + 1) * LANES)
        o_ref[0, :, blk] = (_dot(p, v_bd[kv]) / denom).astype(BF16)

    items = [(kv, m) for kv in range(KV_HEADS) for m in range(pairs)]
    sc = logits(*items[0])
    pending = None
    for idx, item in enumerate(items):
        nxt = logits(*items[idx + 1]) if idx + 1 < len(items) else None
        p, denom = softmax(*item, sc)
        if pending is not None:
            weighted_values(*pending)
        pending = (*item, p, denom)
        sc = nxt
    weighted_values(*pending)


def _attention(q, k, v, kc, vc, sink, band):
    b, s, d = q.shape
    n_ctx = kc.shape[1]
    nkv = kc.shape[2]
    tq = ATTN_BLOCK
    in_specs = [pl.BlockSpec(memory_space=pltpu.SMEM),
                pl.BlockSpec((1, tq, d), lambda i, j: (i, j, 0))]
    args = [sink.astype(F32), q]
    if band:
        in_specs += [pl.BlockSpec((1, s, nkv), lambda i, j: (i, 0, 0))] * 2
        args += [k, v]
    in_specs += [pl.BlockSpec((1, n_ctx, nkv), lambda i, j: (i, 0, 0))] * 2
    args += [kc, vc]
    return pl.pallas_call(
        functools.partial(_attn_kernel, band=band, s_len=s, n_ctx=n_ctx),
        out_shape=jax.ShapeDtypeStruct((b, s, d), BF16),
        grid=(b, s // tq),
        in_specs=in_specs,
        out_specs=pl.BlockSpec((1, tq, d), lambda i, j: (i, j, 0)),
        compiler_params=_params(2),
        name="attn_band" if band else "attn_ctx",
    )(*args)


def _permute_qkv_columns(w_qkv, d):
    w = w_qkv.astype(BF16)
    nkv = KV_HEADS * HEAD_DIM
    wv = w[:, d + nkv:].reshape(-1, KV_HEADS, 1, HEAD_DIM)
    wv = jnp.broadcast_to(wv, (w.shape[0], KV_HEADS, LANES // HEAD_DIM, HEAD_DIM)).reshape(w.shape[0], -1)
    return jnp.concatenate([_to_block_lanes(w[:, :d], 2), _to_block_lanes(w[:, d:d + nkv], 1), wv], axis=1)


def _conv_kernel(h_ref, hp_ref, hn_ref, mod_ref, g_ref, w_in_ref, b_in_ref, dw_ref, dwb_ref, lng_ref,
                       lnb_ref, w_ref, b_ref, o_ref, n_ref, ext_ref, y_ref, *, rows):
    j = pl.program_id(1)
    t, d = h_ref.shape[1], h_ref.shape[2]
    n_chunks = d // LANES
    halo = CONV_HALO
    n_ref[0:halo, :] = _norm_mod(hp_ref[0], g_ref[...], mod_ref, 0).astype(BF16)
    n_ref[halo:halo + t, :] = _norm_mod(h_ref[0], g_ref[...], mod_ref, 0).astype(BF16)
    n_ref[halo + t:, :] = _norm_mod(hn_ref[0], g_ref[...], mod_ref, 0).astype(BF16)
    shift = halo - CONV_WIDTH // 2

    def glu(c):
        a = _dot(n_ref[...], w_in_ref[:, 2 * c * LANES:2 * (c + 1) * LANES]) + b_in_ref[:, 2 * c * LANES:2 * (c + 1) * LANES]
        u = a[:, :LANES] * jax.nn.sigmoid(a[:, LANES:])
        ext_ref[c, 0:halo, :] = jnp.where(j > 0, u[0:halo], 0.0)
        ext_ref[c, halo:halo + t, :] = u[halo:halo + t]
        ext_ref[c, halo + t:, :] = jnp.where(j < pl.num_programs(1) - 1, u[halo + t:], 0.0)

    def conv(c):
        for r0 in range(0, t, rows):
            acc = jnp.zeros((rows, LANES), F32)
            for tap in range(CONV_WIDTH):
                acc = acc + dw_ref[c, tap:tap + 1, :] * ext_ref[c, r0 + tap + shift:r0 + tap + shift + rows, :]
            y_ref[c, r0:r0 + rows, :] = acc

    glu(0)
    for c in range(n_chunks):
        if c + 1 < n_chunks:
            glu(c + 1)
        conv(c)
    y = jnp.concatenate([y_ref[c] for c in range(n_chunks)], axis=1)
    y = _layer_norm(y + dwb_ref[...], lng_ref[...], lnb_ref[...])
    y = (y * jax.nn.sigmoid(y)).astype(BF16)
    acc = _dot(y, w_ref[...]) + b_ref[...]
    o_ref[0] = h_ref[0] + mod_ref[0, 2:3, :] * acc


def _conv_module(h, mod, g, w_in, b_in, dw, dw_b, ln_g, ln_b, w_out, b_out):
    b, s, d = h.shape
    t = _row_tile(s)
    hb = t // CONV_HALO
    last = s // CONV_HALO - 1
    n_chunks = d // LANES
    taps_pad = -(-CONV_WIDTH // SUBLANES) * SUBLANES
    dw_pad = jnp.concatenate([dw, jnp.zeros((taps_pad - CONV_WIDTH, d), F32)], axis=0)
    dw_pad = dw_pad.reshape(taps_pad, n_chunks, LANES).transpose(1, 0, 2)

    def chunk_major(x):
        lead = x.shape[:-1]
        return x.reshape(*lead, 2, n_chunks, LANES).swapaxes(-2, -3).reshape(*lead, 2 * d)

    return pl.pallas_call(
        functools.partial(_conv_kernel, rows=64),
        out_shape=jax.ShapeDtypeStruct(h.shape, F32),
        grid=(b, s // t),
        in_specs=[pl.BlockSpec((1, t, d), lambda i, j: (i, j, 0)),
                  pl.BlockSpec((1, CONV_HALO, d), lambda i, j: (i, jnp.maximum(j * hb - 1, 0), 0)),
                  pl.BlockSpec((1, CONV_HALO, d), lambda i, j: (i, jnp.minimum((j + 1) * hb, last), 0)),
                  pl.BlockSpec((1, 6, d), lambda i, j: (i, 0, 0)),
                  _resident((1, d)), _resident((d, 2 * d)), _resident((1, 2 * d)),
                  _resident((n_chunks, taps_pad, LANES)), _resident((1, d)), _resident((1, d)), _resident((1, d)),
                  _resident((d, d)), _resident((1, d))],
        out_specs=pl.BlockSpec((1, t, d), lambda i, j: (i, j, 0)),
        scratch_shapes=[pltpu.VMEM((t + 2 * CONV_HALO, d), BF16),
                        pltpu.VMEM((n_chunks, t + 2 * CONV_HALO, LANES), F32),
                        pltpu.VMEM((n_chunks, t, LANES), F32)],
        compiler_params=_params(2),
        name="conv",
    )(h, h, h, mod, g.reshape(1, d), chunk_major(w_in.astype(BF16)), chunk_major(b_in.reshape(1, 2 * d)),
      dw_pad, dw_b.reshape(1, d), ln_g.reshape(1, d), ln_b.reshape(1, d), w_out.astype(BF16), b_out.reshape(1, d))


def _gmlp_kernel(h_ref, mod_ref, g_ref, w_in_ref, b_in_ref, lng_ref, lnb_ref, ws_ref, bs_ref, w_out_ref,
                 o_ref, gate_ref, *, sub):
    t, d = h_ref.shape[1], h_ref.shape[2]
    gd = d // GMLP_GROUPS

    def project(r0):
        n = _norm_mod(h_ref[0, r0:r0 + sub, :], g_ref[...], mod_ref, 0).astype(BF16)
        return _dot(n, w_in_ref[...])

    def gate(r0, acc):
        z = jax.nn.gelu(acc + b_in_ref[...])
        u = z[:, :d]
        v = _layer_norm(z[:, d:], lng_ref[...], lnb_ref[...]).astype(BF16)
        for c in range(sub // GMLP_CHUNK):
            rs = slice(c * GMLP_CHUNK, (c + 1) * GMLP_CHUNK)
            for gi in range(GMLP_GROUPS):
                ls = slice(gi * gd, (gi + 1) * gd)
                sg = _dot(ws_ref[gi], v[rs, ls]) + jnp.tile(bs_ref[gi], (1, gd // LANES))
                gate_ref[r0 + c * GMLP_CHUNK:r0 + (c + 1) * GMLP_CHUNK, ls] = (u[rs, ls] * sg).astype(BF16)

    def project_out(r0):
        y = _dot(gate_ref[r0:r0 + sub, :], w_out_ref[...])
        o_ref[0, r0:r0 + sub, :] = h_ref[0, r0:r0 + sub, :] + mod_ref[0, 2:3, :] * y

    acc = project(0)
    for r0 in range(0, t, sub):
        nxt = project(r0 + sub) if r0 + sub < t else None
        gate(r0, acc)
        if r0 > 0:
            project_out(r0 - sub)
        acc = nxt
    project_out(t - sub)


def _chunk_gmlp(h, mod, g, w_in, b_in, ln_g, ln_b, w_s, b_s, w_out):
    b, s, d = h.shape
    t = 1024 if s % 1024 == 0 else (256 if s % 256 == 0 else GMLP_CHUNK)
    bs = jnp.broadcast_to(b_s[:, :, None], (GMLP_GROUPS, GMLP_CHUNK, LANES)).astype(F32)
    return pl.pallas_call(
        functools.partial(_gmlp_kernel, sub=min(t, 256)),
        out_shape=jax.ShapeDtypeStruct(h.shape, F32),
        grid=(b, s // t),
        in_specs=[pl.BlockSpec((1, t, d), lambda i, j: (i, j, 0)),
                  pl.BlockSpec((1, 6, d), lambda i, j: (i, 0, 0)),
                  _resident((1, d)), _resident((d, 2 * d)), _resident((1, 2 * d)),
                  _resident((1, d)), _resident((1, d)),
                  _resident((GMLP_GROUPS, GMLP_CHUNK, GMLP_CHUNK)), _resident((GMLP_GROUPS, GMLP_CHUNK, LANES)),
                  _resident((d, d))],
        out_specs=pl.BlockSpec((1, t, d), lambda i, j: (i, j, 0)),
        scratch_shapes=[pltpu.VMEM((t, d), BF16)],
        compiler_params=_params(2),
        name="gmlp",
    )(h, mod, g.reshape(1, d), w_in.astype(BF16), b_in.reshape(1, 2 * d), ln_g.reshape(1, d),
      ln_b.reshape(1, d), w_s.astype(BF16), bs, w_out.astype(BF16))


def kernel(x, c, ctx, c_ctx, ada_w, ada_b, norm1_g, norm2_g, mlp_w1, mlp_w2, fnet_w, fnet_b, attn_w_qkv, attn_q_g, attn_k_g, attn_sink, attn_w_o, conv_w_in, conv_b_in, conv_dw, conv_dw_b, conv_ln_g, conv_ln_b, conv_w_out, conv_b_out, gmlp_w_in, gmlp_b_in, gmlp_ln_g, gmlp_ln_b, gmlp_w_s, gmlp_b_s, gmlp_w_out):
    bsz, _, d = x.shape
    depth = ada_w.shape[0]
    n_rows = -(-(bsz + 1) // 8) * 8
    c_rows = jnp.concatenate([c, c_ctx[None, :], jnp.zeros((n_rows - bsz - 1, d), F32)], axis=0)
    mods = _adaln(c_rows, ada_w, ada_b).reshape(depth, n_rows, 6, d)

    w1_all = mlp_w1.astype(BF16)
    w2_all = mlp_w2.astype(BF16)
    h_lat, h_ctx = x, ctx
    for i in range(depth):
        last = i == depth - 1
        kind, li = i % N_MIXERS, i // N_MIXERS
        mod_lat = mods[i, :bsz]
        mod_ctx = jnp.broadcast_to(mods[i, bsz], (bsz, 6, d))
        streams = [(h_lat, mod_lat)] + ([] if last else [(h_ctx, mod_ctx)])
        mixed = []
        if kind == 1:
            w_perm = _permute_qkv_columns(attn_w_qkv[li], d)
            w_o = attn_w_o[li].astype(BF16)
            zero_b = jnp.zeros((d,), F32)
            q, k, v = _qkv(h_lat, mod_lat, norm1_g[i], w_perm, attn_q_g[li], attn_k_g[li], True)
            qc, kc, vc = _qkv(h_ctx, mod_ctx, norm1_g[i], w_perm, attn_q_g[li], attn_k_g[li], False)
            mixed.append((h_lat, (_attention(q, k, v, kc, vc, attn_sink[li], True), w_o, zero_b)))
            if not last:
                mixed.append((h_ctx, (_attention(qc, None, None, kc, vc, attn_sink[li], False), w_o, zero_b)))
        else:
            for h, mod in streams:
                if kind == 0:
                    mixed.append((h, (_fourier_mixer(h, mod, norm1_g[i]), fnet_w[li].astype(BF16), fnet_b[li])))
                elif kind == 2:
                    mixed.append((_conv_module(h, mod, norm1_g[i], conv_w_in[li], conv_b_in[li], conv_dw[li],
                                               conv_dw_b[li], conv_ln_g[li], conv_ln_b[li], conv_w_out[li],
                                               conv_b_out[li]), None))
                else:
                    mixed.append((_chunk_gmlp(h, mod, norm1_g[i], gmlp_w_in[li], gmlp_b_in[li], gmlp_ln_g[li],
                                              gmlp_ln_b[li], gmlp_w_s[li], gmlp_b_s[li], gmlp_w_out[li]), None))
        h_lat = _mlp(mixed[0][0], mod_lat, norm2_g[i], w1_all, w2_all, i, mixed[0][1])
        if not last:
            h_ctx = _mlp(mixed[1][0], mod_ctx, norm2_g[i], w1_all, w2_all, i, mixed[1][1])
    return h_lat
```

```python
import functools
import math

import numpy as np
import jax
import jax.numpy as jnp
from jax import lax
from jax.experimental import pallas as pl
from jax.experimental.pallas import tpu as pltpu

F32 = jnp.float32
BF16 = jnp.bfloat16

NORM_EPS = 1e-6
GRID_W = 64
FNET_GROUPS = 4
HEAD_DIM = 64
KV_HEADS = 2
WINDOW = 128
ATTN_BLOCK = 128
ROPE_THETA = 10000.0
ROPE_PAIRS = HEAD_DIM // 4
CONV_WIDTH = 31
CONV_HALO = 16
GMLP_CHUNK = 128
GMLP_GROUPS = 4
N_MIXERS = 4
LANES = 128
SUBLANES = 8
VMEM_LIMIT = 56 * 1024 * 1024
NEG_BIG = -1e30
LOG2E = math.log2(math.e)


def _resident(shape):
    nd = len(shape)
    return pl.BlockSpec(shape, lambda *_: (0,) * nd, pipeline_mode=pl.Buffered(1))


def _params(n_grid):
    return pltpu.CompilerParams(dimension_semantics=("parallel",) * n_grid, vmem_limit_bytes=VMEM_LIMIT)


def _row_tile(s):
    return 512 if s % 512 == 0 else s


def _wide_row_tile(s):
    return 1024 if s % 1024 == 0 else _row_tile(s)


def _dot(a, b):
    return jnp.dot(a, b, preferred_element_type=F32)


def _dot_nt(a, b):
    return lax.dot_general(a, b, (((1,), (1,)), ((), ())), preferred_element_type=F32)


def _norm_mod(x, g, mod_ref, row):
    ms = jnp.mean(x * x, axis=-1, keepdims=True)
    gain = g * (1.0 + mod_ref[0, row + 1:row + 2, :])
    return x * lax.rsqrt(ms + NORM_EPS) * gain + mod_ref[0, row:row + 1, :]


def _pipeline_rows(t, sub, project, finish):
    acc = project(0)
    for r0 in range(0, t, sub):
        nxt = project(r0 + sub) if r0 + sub < t else None
        finish(r0, acc)
        acc = nxt


def _layer_norm(x, g, b):
    mu = jnp.mean(x, axis=-1, keepdims=True)
    xc = x - mu
    var = jnp.mean(xc * xc, axis=-1, keepdims=True)
    return xc * lax.rsqrt(var + NORM_EPS) * g + b


def _adaln_kernel(c_ref, w_ref, b_ref, o_ref):
    c = c_ref[...]
    s = (c * jax.nn.sigmoid(c)).astype(BF16)
    o_ref[0] = _dot(s, w_ref[0].astype(BF16)) + b_ref[0]


def _adaln(c_rows, ada_w, ada_b):
    depth, d, n = ada_w.shape
    r = c_rows.shape[0]
    tn = 1024
    return pl.pallas_call(
        _adaln_kernel,
        out_shape=jax.ShapeDtypeStruct((depth, r, n), F32),
        grid=(depth, n // tn),
        in_specs=[pl.BlockSpec((r, d), lambda l, j: (0, 0)),
                  pl.BlockSpec((1, d, tn), lambda l, j: (l, 0, j)),
                  pl.BlockSpec((1, 1, tn), lambda l, j: (l, 0, j))],
        out_specs=pl.BlockSpec((1, r, tn), lambda l, j: (l, 0, j)),
        compiler_params=_params(2),
        name="adaln",
    )(c_rows, ada_w, ada_b.reshape(depth, 1, n))


def _mlp_kernel(h_ref, mod_ref, g_ref, w1_ref, w2_ref, *refs, ff_chunk):
    x = h_ref[0]
    if len(refs) == 4:
        y_ref, wo_ref, bo_ref, o_ref = refs
        x = x + mod_ref[0, 2:3, :] * (_dot(y_ref[0], wo_ref[...]) + bo_ref[...])
    else:
        o_ref, = refs
    n = _norm_mod(x, g_ref[...], mod_ref, 3).astype(BF16)
    d_ff = w1_ref.shape[2]
    acc = jnp.zeros(x.shape, F32)
    for c in range(d_ff // ff_chunk):
        a = _dot(n, w1_ref[0, :, c * ff_chunk:(c + 1) * ff_chunk])
        a = jnp.square(jnp.maximum(a, 0.0)).astype(BF16)
        acc = acc + _dot(a, w2_ref[0, c * ff_chunk:(c + 1) * ff_chunk, :])
    o_ref[0] = x + mod_ref[0, 5:6, :] * acc


def _mlp(h, mod, g, w1, w2, layer, pending=None):
    b, s, d = h.shape
    t = _wide_row_tile(s)
    d_ff = w1.shape[2]
    in_specs = [pl.BlockSpec((1, t, d), lambda i, j: (i, j, 0)),
                pl.BlockSpec((1, 6, d), lambda i, j: (i, 0, 0)),
                _resident((1, d)),
                pl.BlockSpec((1, d, d_ff), lambda i, j: (layer, 0, 0), pipeline_mode=pl.Buffered(1)),
                pl.BlockSpec((1, d_ff, d), lambda i, j: (layer, 0, 0), pipeline_mode=pl.Buffered(1))]
    args = [h, mod, g.reshape(1, d), w1, w2]
    if pending is not None:
        y, wo, bo = pending
        k = y.shape[-1]
        in_specs += [pl.BlockSpec((1, t, k), lambda i, j: (i, j, 0)), _resident((k, d)), _resident((1, d))]
        args += [y, wo, bo.reshape(1, d)]
    return pl.pallas_call(
        functools.partial(_mlp_kernel, ff_chunk=1024),
        out_shape=jax.ShapeDtypeStruct(h.shape, F32),
        grid=(b, s // t),
        in_specs=in_specs,
        out_specs=pl.BlockSpec((1, t, d), lambda i, j: (i, j, 0)),
        compiler_params=_params(2),
        name="mlp",
    )(*args)


def _scaled_sum(ca, a, cb, b):
    mag = abs(ca)
    if ca > 0 and cb > 0:
        return mag * (a + b)
    if ca > 0:
        return mag * (a - b)
    if cb > 0:
        return mag * (b - a)
    return (-mag) * (a + b)


def _small_dft(z):
    r = len(z)
    if r == 1:
        return z
    even = _small_dft(z[0::2])
    odd = _small_dft(z[1::2])
    out = [None] * r
    for k in range(r // 2):
        (er, ei), (a, b) = even[k], odd[k]
        if k == 0:
            tr, ti = a, b
        elif 4 * k == r:
            out[k] = (er + b, ei - a)
            out[k + r // 2] = (er - b, ei + a)
            continue
        else:
            c, s = math.cos(2 * math.pi * k / r), math.sin(2 * math.pi * k / r)
            if abs(abs(c) - abs(s)) < 1e-12:
                tr, ti = _scaled_sum(c, a, s, b), _scaled_sum(c, b, -s, a)
            else:
                tr, ti = c * a + s * b, c * b - s * a
        out[k] = (er + tr, ei + ti)
        out[k + r // 2] = (er - tr, ei - ti)
    return out


def _fnet_chan_kernel(*refs, gc, radix):
    h_refs = refs[:radix]
    mod_ref, g_ref, cs_ref, tw_ref, yr_ref, yi_ref, n_ref = refs[radix:]
    d = h_refs[0].shape[2]
    for q in range(radix):
        n_ref[q] = _norm_mod(h_refs[q][0], g_ref[...], mod_ref, 0).astype(BF16)
    for gi in range(d // gc):
        ls = slice(gi * gc, (gi + 1) * gc)
        z = []
        for q in range(radix):
            r = _dot(n_ref[q, :, ls], cs_ref[...])
            z.append((r[:, :gc], r[:, gc:]))
        for p, (br, bi) in enumerate(_small_dft(z)):
            if p > 0:
                c = jnp.tile(tw_ref[p - 1, 0], (1, gc // LANES))
                s = jnp.tile(tw_ref[p - 1, 1], (1, gc // LANES))
                br, bi = c * br + s * bi, c * bi - s * br
            yr_ref[0, p, :, ls] = br.astype(BF16)
            yi_ref[0, p, :, ls] = bi.astype(BF16)


def _fnet_pos_kernel(yr_ref, yi_ref, csm_ref, o_ref, x_ref, *, radix, m):
    tc = yr_ref.shape[3]
    for p in range(radix):
        x = _dot(csm_ref[...], jnp.concatenate([yr_ref[0, p], yi_ref[0, p]], axis=0))
        if radix == 1:
            o_ref[0] = x.astype(BF16)
        else:
            for cb in range(tc // LANES):
                x_ref[cb, pl.ds(p, m, stride=radix), :] = x[:, cb * LANES:(cb + 1) * LANES]
    if radix > 1:
        for cb in range(tc // LANES):
            o_ref[0, :, cb * LANES:(cb + 1) * LANES] = x_ref[cb].astype(BF16)


def _fnet_tables(s, gc):
    radix = 8 if s % 1024 == 0 else 1
    m = s // radix
    ang_c = 2.0 * np.pi * np.outer(np.arange(gc), np.arange(gc)) / gc
    cs = np.concatenate([np.cos(ang_c), -np.sin(ang_c)], axis=1)
    ang_m = 2.0 * np.pi * np.outer(np.arange(m), np.arange(m)) / m
    csm = np.concatenate([np.cos(ang_m), np.sin(ang_m)], axis=1) / math.sqrt(s * gc)
    if radix > 1:
        ang_t = 2.0 * np.pi * np.outer(np.arange(1, radix), np.arange(m)) / s
        tw = np.stack([np.cos(ang_t), np.sin(ang_t)], axis=1)
        tw = np.broadcast_to(tw[..., None], (radix - 1, 2, m, LANES))
    else:
        tw = np.zeros((1, 2, m, LANES))
    return (radix, m, jnp.asarray(cs, BF16), jnp.asarray(csm, BF16), jnp.asarray(tw, F32))


def _fourier_mixer(h, mod, g):
    b, s, d = h.shape
    gc = d // FNET_GROUPS
    radix, m, cs, csm, tw = _fnet_tables(s, gc)
    tm = min(m, 128)
    steps = m // tm
    h_specs = [pl.BlockSpec((1, tm, d), lambda i, j, q=q: (i, q * steps + j, 0)) for q in range(radix)]
    y_spec = pl.BlockSpec((1, radix, tm, d), lambda i, j: (i, 0, j, 0))
    yr, yi = pl.pallas_call(
        functools.partial(_fnet_chan_kernel, gc=gc, radix=radix),
        out_shape=(jax.ShapeDtypeStruct((b, radix, m, d), BF16),) * 2,
        grid=(b, steps),
        in_specs=h_specs + [pl.BlockSpec((1, 6, d), lambda i, j: (i, 0, 0)),
                            _resident((1, d)), _resident((gc, 2 * gc)),
                            pl.BlockSpec((tw.shape[0], 2, tm, LANES), lambda i, j: (0, 0, j, 0))],
        out_specs=(y_spec, y_spec),
        scratch_shapes=[pltpu.VMEM((radix, tm, d), BF16)],
        compiler_params=_params(2),
        name="fnet_chan",
    )(*([h] * radix), mod, g.reshape(1, d), cs, tw)
    tc = 256
    return pl.pallas_call(
        functools.partial(_fnet_pos_kernel, radix=radix, m=m),
        out_shape=jax.ShapeDtypeStruct((b, s, d), BF16),
        grid=(b, d // tc),
        in_specs=[pl.BlockSpec((1, radix, m, tc), lambda i, j: (i, 0, 0, j)),
                  pl.BlockSpec((1, radix, m, tc), lambda i, j: (i, 0, 0, j)),
                  _resident((m, 2 * m))],
        out_specs=pl.BlockSpec((1, s, tc), lambda i, j: (i, 0, j)),
        scratch_shapes=[pltpu.VMEM((tc // LANES, s if radix > 1 else SUBLANES, LANES), F32)],
        compiler_params=_params(2),
        name="fnet_pos",
    )(yr, yi, csm)


def _qkv_kernel(h_ref, mod_ref, g_ref, w_ref, same_ref, hg_ref, cos_ref, sin_ref,
                q_ref, k_ref, v_ref, *, nq, nk, sub):
    t = h_ref.shape[1]

    def project(r0):
        n = _norm_mod(h_ref[0, r0:r0 + sub, :], g_ref[...], mod_ref, 0).astype(BF16)
        return _dot(n, w_ref[...])

    def epilogue(r0, acc):
        qk = acc[:, :nq + nk]
        sq = (qk * qk).astype(BF16)
        width = same_ref.shape[0]
        ssq = jnp.concatenate([_dot(sq[:, c:c + width], same_ref[...]) for c in range(0, nq + nk, width)], axis=1)
        qk = qk * lax.rsqrt(ssq * (1.0 / HEAD_DIM) + NORM_EPS) * hg_ref[...]
        cos = cos_ref[r0:r0 + sub, :]
        sin = sin_ref[r0:r0 + sub, :]
        for c in range((nq + nk) // LANES):
            x = qk[:, c * LANES:(c + 1) * LANES]
            y = (x * cos + pltpu.roll(x, LANES // 2, 1) * sin).astype(BF16)
            if c < nq // LANES:
                q_ref[0, r0:r0 + sub, c * LANES:(c + 1) * LANES] = y
            else:
                k_ref[0, r0:r0 + sub, c * LANES - nq:(c + 1) * LANES - nq] = y
        v_ref[0, r0:r0 + sub, :] = acc[:, nq + nk:].astype(BF16)

    _pipeline_rows(t, sub, project, epilogue)


def _to_block_lanes(x, slots):
    lead = x.shape[:-1]
    x = x.reshape(*lead, -1, slots, 2, 2, ROPE_PAIRS)
    x = jnp.moveaxis(x, -2, -4)
    if slots == 1:
        x = jnp.broadcast_to(x, (*x.shape[:-3], 2, 2, ROPE_PAIRS))
    return x.reshape(*lead, -1)


def _rope_tables(s, with_rope):
    if not with_rope:
        return jnp.ones((s, LANES), F32), jnp.zeros((s, LANES), F32)
    rows = s // GRID_W
    row = jnp.repeat(jnp.arange(rows, dtype=F32), GRID_W)
    col = jnp.tile(jnp.arange(GRID_W, dtype=F32), rows)
    inv = ROPE_THETA ** (-jnp.arange(ROPE_PAIRS, dtype=F32) / ROPE_PAIRS)
    ang_r = row[:, None] * inv[None, :]
    ang_c = col[:, None] * inv[None, :]
    cos = jnp.concatenate([jnp.cos(ang_r), jnp.cos(ang_c)], axis=1)
    sin = jnp.concatenate([jnp.sin(ang_r), jnp.sin(ang_c)], axis=1)
    reps = LANES // (4 * ROPE_PAIRS)
    return jnp.tile(cos, (1, 2 * reps)), jnp.concatenate([jnp.tile(-sin, (1, reps)), jnp.tile(sin, (1, reps))], axis=1)


def _qkv(h, mod, g, w_perm, q_g, k_g, with_rope):
    b, s, d = h.shape
    nq = d
    nk = KV_HEADS * LANES
    t = _wide_row_tile(s)
    width = 2 * LANES
    lane = np.arange(width)
    head = 2 * (lane // LANES) + (lane % HEAD_DIM) // (HEAD_DIM // 2)
    same = (head[:, None] == head[None, :]).astype(np.float32)
    assert (nq + nk) % width == 0
    q_gain = _to_block_lanes(jnp.tile(q_g * (HEAD_DIM ** -0.5 * LOG2E), 2), 2)
    hg = jnp.concatenate([jnp.tile(q_gain, nq // LANES), jnp.tile(_to_block_lanes(k_g, 1), nk // LANES)])
    hg = hg.reshape(1, nq + nk)
    cos, sin = _rope_tables(s, with_rope)
    return pl.pallas_call(
        functools.partial(_qkv_kernel, nq=nq, nk=nk, sub=min(t, 256)),
        out_shape=(jax.ShapeDtypeStruct((b, s, nq), BF16),
                   jax.ShapeDtypeStruct((b, s, nk), BF16),
                   jax.ShapeDtypeStruct((b, s, nk), BF16)),
        grid=(b, s // t),
        in_specs=[pl.BlockSpec((1, t, d), lambda i, j: (i, j, 0)),
                  pl.BlockSpec((1, 6, d), lambda i, j: (i, 0, 0)),
                  _resident((1, d)), _resident(w_perm.shape),
                  _resident((width, width)), _resident((1, nq + nk)),
                  pl.BlockSpec((t, LANES), lambda i, j: (j, 0)),
                  pl.BlockSpec((t, LANES), lambda i, j: (j, 0))],
        out_specs=(pl.BlockSpec((1, t, nq), lambda i, j: (i, j, 0)),
                   pl.BlockSpec((1, t, nk), lambda i, j: (i, j, 0)),
                   pl.BlockSpec((1, t, nk), lambda i, j: (i, j, 0))),
        compiler_params=_params(2),
        name="qkv",
    )(h, mod, g.reshape(1, d), w_perm, jnp.asarray(same, BF16), hg, cos, sin)


def _attn_kernel(sink_ref, q_ref, *refs, band, s_len, n_ctx):
    if band:
        k_ref, v_ref, kc_ref, vc_ref, o_ref = refs
    else:
        kc_ref, vc_ref, o_ref = refs
    nq = ATTN_BLOCK
    n_blocks = q_ref.shape[1] // nq
    n_band = 3 * ATTN_BLOCK if band else 0
    n_keys = n_band + n_ctx
    group = q_ref.shape[2] // (KV_HEADS * HEAD_DIM)
    pairs = group // 2
    lane = lax.broadcasted_iota(jnp.int32, (1, LANES), 1)
    low = lane < HEAD_DIM
    first = (lane % HEAD_DIM) < HEAD_DIM // 2

    bias_band, k_bd, v_bd = [], [], []
    for qb in range(n_blocks):
        j = pl.program_id(1) * n_blocks + qb
        if band:
            start = pl.multiple_of(jnp.clip((j - 1) * ATTN_BLOCK, 0, s_len - n_band), ATTN_BLOCK)
            qpos = j * ATTN_BLOCK + lax.broadcasted_iota(jnp.int32, (nq, n_band), 0)
            kpos = start + lax.broadcasted_iota(jnp.int32, (nq, n_band), 1)
            bias_band.append(jnp.where(jnp.abs(qpos - kpos) <= WINDOW, 0.0, NEG_BIG).astype(F32))
        for kv in range(KV_HEADS):
            lanes_kv = slice(kv * LANES, (kv + 1) * LANES)
            k_all = kc_ref[0, :, lanes_kv]
            v_all = vc_ref[0, :, lanes_kv]
            if band:
                k_all = jnp.concatenate([k_ref[0, pl.ds(start, n_band), lanes_kv], k_all], axis=0)
                v_all = jnp.concatenate([v_ref[0, pl.ds(start, n_band), lanes_kv], v_all], axis=0)
            zero = jnp.zeros_like(k_all)
            k_bd.append(jnp.concatenate([jnp.where(first, k_all, zero), jnp.where(first, zero, k_all)], axis=0))
            v_bd.append(jnp.concatenate([jnp.where(low, v_all, zero), jnp.where(low, zero, v_all)], axis=0))

    def logits(qb, kv, m):
        blk = slice((kv * pairs + m) * LANES, (kv * pairs + m + 1) * LANES)
        return _dot_nt(q_ref[0, qb * nq:(qb + 1) * nq, blk], k_bd[qb * KV_HEADS + kv])

    def softmax(qb, kv, m, sc):
        probs, denoms = [], []
        for half in range(2):
            sh = sc[:, half * n_keys:(half + 1) * n_keys]
            if band:
                sh = jnp.concatenate([sh[:, :n_band] + bias_band[qb], sh[:, n_band:]], axis=1)
            sink = sink_ref[kv * group + 2 * m + half] * LOG2E
            mx = jnp.maximum(jnp.max(sh, axis=1, keepdims=True), sink)
            e = jnp.exp2(sh - mx)
            denoms.append(jnp.sum(e, axis=1, keepdims=True) + jnp.exp2(sink - mx))
            probs.append(e.astype(BF16))
        return jnp.concatenate(probs, axis=1), jnp.where(low, denoms[0], denoms[1])

    def weighted_values(qb, kv, m, p, denom):
        blk = slice((kv * pairs + m) * LANES, (kv * pairs + m + 1) * LANES)
        o_ref[0, qb * nq:(qb + 1) * nq, blk] = (_dot(p, v_bd[qb * KV_HEADS + kv]) / denom).astype(BF16)

    items = [(qb, kv, m) for qb in range(n_blocks) for kv in range(KV_HEADS) for m in range(pairs)]
    sc = logits(*items[0])
    pending = None
    for idx, item in enumerate(items):
        nxt = logits(*items[idx + 1]) if idx + 1 < len(items) else None
        p, denom = softmax(*item, sc)
        if pending is not None:
            weighted_values(*pending)
        pending = (*item, p, denom)
        sc = nxt
    weighted_values(*pending)


def _attention(q, k, v, kc, vc, sink, band):
    b, s, d = q.shape
    n_ctx = kc.shape[1]
    nkv = kc.shape[2]
    tq = 4 * ATTN_BLOCK if s % (4 * ATTN_BLOCK) == 0 else (2 * ATTN_BLOCK if s % (2 * ATTN_BLOCK) == 0 else ATTN_BLOCK)
    in_specs = [pl.BlockSpec(memory_space=pltpu.SMEM),
                pl.BlockSpec((1, tq, d), lambda i, j: (i, j, 0))]
    args = [sink.astype(F32), q]
    if band:
        in_specs += [pl.BlockSpec((1, s, nkv), lambda i, j: (i, 0, 0))] * 2
        args += [k, v]
    in_specs += [pl.BlockSpec((1, n_ctx, nkv), lambda i, j: (i, 0, 0))] * 2
    args += [kc, vc]
    return pl.pallas_call(
        functools.partial(_attn_kernel, band=band, s_len=s, n_ctx=n_ctx),
        out_shape=jax.ShapeDtypeStruct((b, s, d), BF16),
        grid=(b, s // tq),
        in_specs=in_specs,
        out_specs=pl.BlockSpec((1, tq, d), lambda i, j: (i, j, 0)),
        compiler_params=_params(2),
        name="attn_band" if band else "attn_ctx",
    )(*args)


def _permute_qkv_columns(w_qkv, d):
    w = w_qkv.astype(BF16)
    nkv = KV_HEADS * HEAD_DIM
    wv = w[:, d + nkv:].reshape(-1, KV_HEADS, 1, HEAD_DIM)
    wv = jnp.broadcast_to(wv, (w.shape[0], KV_HEADS, LANES // HEAD_DIM, HEAD_DIM)).reshape(w.shape[0], -1)
    return jnp.concatenate([_to_block_lanes(w[:, :d], 2), _to_block_lanes(w[:, d:d + nkv], 1), wv], axis=1)


def _conv_kernel(h_ref, hp_ref, hn_ref, mod_ref, g_ref, w_in_ref, b_in_ref, dw_ref, dwb_ref, lng_ref,
                       lnb_ref, w_ref, b_ref, o_ref, n_ref, ext_ref, y_ref, *, rows):
    j = pl.program_id(1)
    t, d = h_ref.shape[1], h_ref.shape[2]
    n_chunks = d // LANES
    halo = CONV_HALO
    n_ref[0:halo, :] = _norm_mod(hp_ref[0], g_ref[...], mod_ref, 0).astype(BF16)
    n_ref[halo:halo + t, :] = _norm_mod(h_ref[0], g_ref[...], mod_ref, 0).astype(BF16)
    n_ref[halo + t:, :] = _norm_mod(hn_ref[0], g_ref[...], mod_ref, 0).astype(BF16)
    shift = halo - CONV_WIDTH // 2

    def glu(c):
        a = _dot(n_ref[...], w_in_ref[:, 2 * c * LANES:2 * (c + 1) * LANES]) + b_in_ref[:, 2 * c * LANES:2 * (c + 1) * LANES]
        u = a[:, :LANES] * jax.nn.sigmoid(a[:, LANES:])
        ext_ref[c, 0:halo, :] = jnp.where(j > 0, u[0:halo], 0.0)
        ext_ref[c, halo:halo + t, :] = u[halo:halo + t]
        ext_ref[c, halo + t:, :] = jnp.where(j < pl.num_programs(1) - 1, u[halo + t:], 0.0)

    def conv(c):
        for r0 in range(0, t, rows):
            acc = jnp.zeros((rows, LANES), F32)
            for tap in range(CONV_WIDTH):
                acc = acc + dw_ref[c, tap:tap + 1, :] * ext_ref[c, r0 + tap + shift:r0 + tap + shift + rows, :]
            y_ref[c, r0:r0 + rows, :] = acc

    glu(0)
    for c in range(n_chunks):
        if c + 1 < n_chunks:
            glu(c + 1)
        conv(c)
    y = jnp.concatenate([y_ref[c] for c in range(n_chunks)], axis=1)
    y = _layer_norm(y + dwb_ref[...], lng_ref[...], lnb_ref[...])
    y = (y * jax.nn.sigmoid(y)).astype(BF16)
    acc = _dot(y, w_ref[...]) + b_ref[...]
    o_ref[0] = h_ref[0] + mod_ref[0, 2:3, :] * acc


def _conv_module(h, mod, g, w_in, b_in, dw, dw_b, ln_g, ln_b, w_out, b_out):
    b, s, d = h.shape
    t = _row_tile(s)
    hb = t // CONV_HALO
    last = s // CONV_HALO - 1
    n_chunks = d // LANES
    taps_pad = -(-CONV_WIDTH // SUBLANES) * SUBLANES
    dw_pad = jnp.concatenate([dw, jnp.zeros((taps_pad - CONV_WIDTH, d), F32)], axis=0)
    dw_pad = dw_pad.reshape(taps_pad, n_chunks, LANES).transpose(1, 0, 2)

    def chunk_major(x):
        lead = x.shape[:-1]
        return x.reshape(*lead, 2, n_chunks, LANES).swapaxes(-2, -3).reshape(*lead, 2 * d)

    return pl.pallas_call(
        functools.partial(_conv_kernel, rows=64),
        out_shape=jax.ShapeDtypeStruct(h.shape, F32),
        grid=(b, s // t),
        in_specs=[pl.BlockSpec((1, t, d), lambda i, j: (i, j, 0)),
                  pl.BlockSpec((1, CONV_HALO, d), lambda i, j: (i, jnp.maximum(j * hb - 1, 0), 0)),
                  pl.BlockSpec((1, CONV_HALO, d), lambda i, j: (i, jnp.minimum((j + 1) * hb, last), 0)),
                  pl.BlockSpec((1, 6, d), lambda i, j: (i, 0, 0)),
                  _resident((1, d)), _resident((d, 2 * d)), _resident((1, 2 * d)),
                  _resident((n_chunks, taps_pad, LANES)), _resident((1, d)), _resident((1, d)), _resident((1, d)),
                  _resident((d, d)), _resident((1, d))],
        out_specs=pl.BlockSpec((1, t, d), lambda i, j: (i, j, 0)),
        scratch_shapes=[pltpu.VMEM((t + 2 * CONV_HALO, d), BF16),
                        pltpu.VMEM((n_chunks, t + 2 * CONV_HALO, LANES), F32),
                        pltpu.VMEM((n_chunks, t, LANES), F32)],
        compiler_params=_params(2),
        name="conv",
    )(h, h, h, mod, g.reshape(1, d), chunk_major(w_in.astype(BF16)), chunk_major(b_in.reshape(1, 2 * d)),
      dw_pad, dw_b.reshape(1, d), ln_g.reshape(1, d), ln_b.reshape(1, d), w_out.astype(BF16), b_out.reshape(1, d))


def _gmlp_kernel(h_ref, mod_ref, g_ref, w_in_ref, b_in_ref, lng_ref, lnb_ref, ws_ref, bs_ref, w_out_ref,
                 o_ref, gate_ref, *, sub):
    t, d = h_ref.shape[1], h_ref.shape[2]
    gd = d // GMLP_GROUPS

    def project(r0):
        n = _norm_mod(h_ref[0, r0:r0 + sub, :], g_ref[...], mod_ref, 0).astype(BF16)
        return _dot(n, w_in_ref[...])

    def gate(r0, acc):
        z = jax.nn.gelu(acc + b_in_ref[...])
        u = z[:, :d]
        v = _layer_norm(z[:, d:], lng_ref[...], lnb_ref[...]).astype(BF16)
        for c in range(sub // GMLP_CHUNK):
            rs = slice(c * GMLP_CHUNK, (c + 1) * GMLP_CHUNK)
            for gi in range(GMLP_GROUPS):
                ls = slice(gi * gd, (gi + 1) * gd)
                sg = _dot(ws_ref[gi], v[rs, ls]) + jnp.tile(bs_ref[gi], (1, gd // LANES))
                gate_ref[r0 + c * GMLP_CHUNK:r0 + (c + 1) * GMLP_CHUNK, ls] = (u[rs, ls] * sg).astype(BF16)

    def project_out(r0):
        y = _dot(gate_ref[r0:r0 + sub, :], w_out_ref[...])
        o_ref[0, r0:r0 + sub, :] = h_ref[0, r0:r0 + sub, :] + mod_ref[0, 2:3, :] * y

    acc = project(0)
    for r0 in range(0, t, sub):
        nxt = project(r0 + sub) if r0 + sub < t else None
        gate(r0, acc)
        if r0 > 0:
            project_out(r0 - sub)
        acc = nxt
    project_out(t - sub)


def _chunk_gmlp(h, mod, g, w_in, b_in, ln_g, ln_b, w_s, b_s, w_out):
    b, s, d = h.shape
    t = 1024 if s % 1024 == 0 else (256 if s % 256 == 0 else GMLP_CHUNK)
    bs = jnp.broadcast_to(b_s[:, :, None], (GMLP_GROUPS, GMLP_CHUNK, LANES)).astype(F32)
    return pl.pallas_call(
        functools.partial(_gmlp_kernel, sub=min(t, 256)),
        out_shape=jax.ShapeDtypeStruct(h.shape, F32),
        grid=(b, s // t),
        in_specs=[pl.BlockSpec((1, t, d), lambda i, j: (i, j, 0)),
                  pl.BlockSpec((1, 6, d), lambda i, j: (i, 0, 0)),
                  _resident((1, d)), _resident((d, 2 * d)), _resident((1, 2 * d)),
                  _resident((1, d)), _resident((1, d)),
                  _resident((GMLP_GROUPS, GMLP_CHUNK, GMLP_CHUNK)), _resident((GMLP_GROUPS, GMLP_CHUNK, LANES)),
                  _resident((d, d))],
        out_specs=pl.BlockSpec((1, t, d), lambda i, j: (i, j, 0)),
        scratch_shapes=[pltpu.VMEM((t, d), BF16)],
        compiler_params=_params(2),
        name="gmlp",
    )(h, mod, g.reshape(1, d), w_in.astype(BF16), b_in.reshape(1, 2 * d), ln_g.reshape(1, d),
      ln_b.reshape(1, d), w_s.astype(BF16), bs, w_out.astype(BF16))


def kernel(x, c, ctx, c_ctx, ada_w, ada_b, norm1_g, norm2_g, mlp_w1, mlp_w2, fnet_w, fnet_b, attn_w_qkv, attn_q_g, attn_k_g, attn_sink, attn_w_o, conv_w_in, conv_b_in, conv_dw, conv_dw_b, conv_ln_g, conv_ln_b, conv_w_out, conv_b_out, gmlp_w_in, gmlp_b_in, gmlp_ln_g, gmlp_ln_b, gmlp_w_s, gmlp_b_s, gmlp_w_out):
    bsz, _, d = x.shape
    depth = ada_w.shape[0]
    n_rows = -(-(bsz + 1) // 8) * 8
    c_rows = jnp.concatenate([c, c_ctx[None, :], jnp.zeros((n_rows - bsz - 1, d), F32)], axis=0)
    mods = _adaln(c_rows, ada_w, ada_b).reshape(depth, n_rows, 6, d)

    w1_all = mlp_w1.astype(BF16)
    w2_all = mlp_w2.astype(BF16)
    h_lat, h_ctx = x, ctx
    for i in range(depth):
        last = i == depth - 1
        kind, li = i % N_MIXERS, i // N_MIXERS
        mod_lat = mods[i, :bsz]
        mod_ctx = jnp.broadcast_to(mods[i, bsz], (bsz, 6, d))
        streams = [(h_lat, mod_lat)] + ([] if last else [(h_ctx, mod_ctx)])
        mixed = []
        if kind == 1:
            w_perm = _permute_qkv_columns(attn_w_qkv[li], d)
            w_o = attn_w_o[li].astype(BF16)
            zero_b = jnp.zeros((d,), F32)
            q, k, v = _qkv(h_lat, mod_lat, norm1_g[i], w_perm, attn_q_g[li], attn_k_g[li], True)
            qc, kc, vc = _qkv(h_ctx, mod_ctx, norm1_g[i], w_perm, attn_q_g[li], attn_k_g[li], False)
            mixed.append((h_lat, (_attention(q, k, v, kc, vc, attn_sink[li], True), w_o, zero_b)))
            if not last:
                mixed.append((h_ctx, (_attention(qc, None, None, kc, vc, attn_sink[li], False), w_o, zero_b)))
        else:
            for h, mod in streams:
                if kind == 0:
                    mixed.append((h, (_fourier_mixer(h, mod, norm1_g[i]), fnet_w[li].astype(BF16), fnet_b[li])))
                elif kind == 2:
                    mixed.append((_conv_module(h, mod, norm1_g[i], conv_w_in[li], conv_b_in[li], conv_dw[li],
                                               conv_dw_b[li], conv_ln_g[li], conv_ln_b[li], conv_w_out[li],
                                               conv_b_out[li]), None))
                else:
                    mixed.append((_chunk_gmlp(h, mod, norm1_g[i], gmlp_w_in[li], gmlp_b_in[li], gmlp_ln_g[li],
                                              gmlp_ln_b[li], gmlp_w_s[li], gmlp_b_s[li], gmlp_w_out[li]), None))
        h_lat = _mlp(mixed[0][0], mod_lat, norm2_g[i], w1_all, w2_all, i, mixed[0][1])
        if not last:
            h_ctx = _mlp(mixed[1][0], mod_ctx, norm2_g[i], w1_all, w2_all, i, mixed[1][1])
    return h_lat
```

```python
import functools
import math

import numpy as np
import jax
import jax.numpy as jnp
from jax import lax
from jax.experimental import pallas as pl
from jax.experimental.pallas import tpu as pltpu

F32 = jnp.float32
BF16 = jnp.bfloat16

NORM_EPS = 1e-6
GRID_W = 64
FNET_GROUPS = 4
HEAD_DIM = 64
KV_HEADS = 2
WINDOW = 128
ATTN_BLOCK = 128
ROPE_THETA = 10000.0
ROPE_PAIRS = HEAD_DIM // 4
CONV_WIDTH = 31
CONV_HALO = 16
GMLP_CHUNK = 128
GMLP_GROUPS = 4
N_MIXERS = 4
LANES = 128
SUBLANES = 8
VMEM_LIMIT = 56 * 1024 * 1024
NEG_BIG = -1e30
LOG2E = math.log2(math.e)


def _resident(shape):
    nd = len(shape)
    return pl.BlockSpec(shape, lambda *_: (0,) * nd, pipeline_mode=pl.Buffered(1))


def _params(n_grid):
    return pltpu.CompilerParams(dimension_semantics=("parallel",) * n_grid, vmem_limit_bytes=VMEM_LIMIT)


def _row_tile(s):
    return 512 if s % 512 == 0 else s


def _wide_row_tile(s):
    return 1024 if s % 1024 == 0 else _row_tile(s)


def _dot(a, b):
    return jnp.dot(a, b, preferred_element_type=F32)


def _dot_nt(a, b):
    return lax.dot_general(a, b, (((1,), (1,)), ((), ())), preferred_element_type=F32)


def _norm_mod(x, g, mod_ref, row):
    ms = jnp.mean(x * x, axis=-1, keepdims=True)
    gain = g * (1.0 + mod_ref[0, row + 1:row + 2, :])
    return x * lax.rsqrt(ms + NORM_EPS) * gain + mod_ref[0, row:row + 1, :]


def _pipeline_rows(t, sub, project, finish):
    acc = project(0)
    for r0 in range(0, t, sub):
        nxt = project(r0 + sub) if r0 + sub < t else None
        finish(r0, acc)
        acc = nxt


def _layer_norm(x, g, b):
    mu = jnp.mean(x, axis=-1, keepdims=True)
    xc = x - mu
    var = jnp.mean(xc * xc, axis=-1, keepdims=True)
    return xc * lax.rsqrt(var + NORM_EPS) * g + b


def _adaln_kernel(c_ref, w_ref, b_ref, o_ref):
    c = c_ref[...]
    s = (c * jax.nn.sigmoid(c)).astype(BF16)
    o_ref[0] = _dot(s, w_ref[0].astype(BF16)) + b_ref[0]


def _adaln(c_rows, ada_w, ada_b):
    depth, d, n = ada_w.shape
    r = c_rows.shape[0]
    tn = 1024
    return pl.pallas_call(
        _adaln_kernel,
        out_shape=jax.ShapeDtypeStruct((depth, r, n), F32),
        grid=(depth, n // tn),
        in_specs=[pl.BlockSpec((r, d), lambda l, j: (0, 0)),
                  pl.BlockSpec((1, d, tn), lambda l, j: (l, 0, j)),
                  pl.BlockSpec((1, 1, tn), lambda l, j: (l, 0, j))],
        out_specs=pl.BlockSpec((1, r, tn), lambda l, j: (l, 0, j)),
        compiler_params=_params(2),
        name="adaln",
    )(c_rows, ada_w, ada_b.reshape(depth, 1, n))


def _mlp_kernel(h_ref, mod_ref, g_ref, w1_ref, w2_ref, *refs, ff_chunk):
    x = h_ref[0]
    if len(refs) == 4:
        y_ref, wo_ref, bo_ref, o_ref = refs
        x = x + mod_ref[0, 2:3, :] * (_dot(y_ref[0], wo_ref[...]) + bo_ref[...])
    else:
        o_ref, = refs
    n = _norm_mod(x, g_ref[...], mod_ref, 3).astype(BF16)
    d_ff = w1_ref.shape[2]
    acc = jnp.zeros(x.shape, F32)
    for c in range(d_ff // ff_chunk):
        a = _dot(n, w1_ref[0, :, c * ff_chunk:(c + 1) * ff_chunk])
        a = jnp.square(jnp.maximum(a, 0.0)).astype(BF16)
        acc = acc + _dot(a, w2_ref[0, c * ff_chunk:(c + 1) * ff_chunk, :])
    o_ref[0] = x + mod_ref[0, 5:6, :] * acc


def _mlp(h, mod, g, w1, w2, layer, pending=None):
    b, s, d = h.shape
    t = _wide_row_tile(s)
    d_ff = w1.shape[2]
    in_specs = [pl.BlockSpec((1, t, d), lambda i, j: (i, j, 0)),
                pl.BlockSpec((1, 6, d), lambda i, j: (i, 0, 0)),
                _resident((1, d)),
                pl.BlockSpec((1, d, d_ff), lambda i, j: (layer, 0, 0), pipeline_mode=pl.Buffered(1)),
                pl.BlockSpec((1, d_ff, d), lambda i, j: (layer, 0, 0), pipeline_mode=pl.Buffered(1))]
    args = [h, mod, g.reshape(1, d), w1, w2]
    if pending is not None:
        y, wo, bo = pending
        k = y.shape[-1]
        in_specs += [pl.BlockSpec((1, t, k), lambda i, j: (i, j, 0)), _resident((k, d)), _resident((1, d))]
        args += [y, wo, bo.reshape(1, d)]
    return pl.pallas_call(
        functools.partial(_mlp_kernel, ff_chunk=1024),
        out_shape=jax.ShapeDtypeStruct(h.shape, F32),
        grid=(b, s // t),
        in_specs=in_specs,
        out_specs=pl.BlockSpec((1, t, d), lambda i, j: (i, j, 0)),
        compiler_params=_params(2),
        name="mlp",
    )(*args)


def _scaled_sum(ca, a, cb, b):
    mag = abs(ca)
    if ca > 0 and cb > 0:
        return mag * (a + b)
    if ca > 0:
        return mag * (a - b)
    if cb > 0:
        return mag * (b - a)
    return (-mag) * (a + b)


def _small_dft(z):
    r = len(z)
    if r == 1:
        return z
    even = _small_dft(z[0::2])
    odd = _small_dft(z[1::2])
    out = [None] * r
    for k in range(r // 2):
        (er, ei), (a, b) = even[k], odd[k]
        if k == 0:
            tr, ti = a, b
        elif 4 * k == r:
            out[k] = (er + b, ei - a)
            out[k + r // 2] = (er - b, ei + a)
            continue
        else:
            c, s = math.cos(2 * math.pi * k / r), math.sin(2 * math.pi * k / r)
            if abs(abs(c) - abs(s)) < 1e-12:
                tr, ti = _scaled_sum(c, a, s, b), _scaled_sum(c, b, -s, a)
            else:
                tr, ti = c * a + s * b, c * b - s * a
        out[k] = (er + tr, ei + ti)
        out[k + r // 2] = (er - tr, ei - ti)
    return out


def _fnet_chan_kernel(*refs, gc, radix):
    h_refs = refs[:radix]
    mod_ref, g_ref, cs_ref, tw_ref, yr_ref, yi_ref, n_ref = refs[radix:]
    d = h_refs[0].shape[2]
    for q in range(radix):
        n_ref[q] = _norm_mod(h_refs[q][0], g_ref[...], mod_ref, 0).astype(BF16)
    for gi in range(d // gc):
        ls = slice(gi * gc, (gi + 1) * gc)
        z = []
        for q in range(radix):
            r = _dot(n_ref[q, :, ls], cs_ref[...])
            z.append((r[:, :gc], r[:, gc:]))
        for p, (br, bi) in enumerate(_small_dft(z)):
            if p > 0:
                c = jnp.tile(tw_ref[p - 1, 0], (1, gc // LANES))
                s = jnp.tile(tw_ref[p - 1, 1], (1, gc // LANES))
                br, bi = c * br + s * bi, c * bi - s * br
            yr_ref[0, p, :, ls] = br.astype(BF16)
            yi_ref[0, p, :, ls] = bi.astype(BF16)


def _fnet_pos_kernel(yr_ref, yi_ref, csm_ref, o_ref, x_ref, *, radix, m):
    tc = yr_ref.shape[3]
    for p in range(radix):
        x = _dot(csm_ref[...], jnp.concatenate([yr_ref[0, p], yi_ref[0, p]], axis=0))
        if radix == 1:
            o_ref[0] = x.astype(BF16)
        else:
            for cb in range(tc // LANES):
                x_ref[cb, pl.ds(p, m, stride=radix), :] = x[:, cb * LANES:(cb + 1) * LANES]
    if radix > 1:
        for cb in range(tc // LANES):
            o_ref[0, :, cb * LANES:(cb + 1) * LANES] = x_ref[cb].astype(BF16)


def _fnet_tables(s, gc):
    radix = 8 if s % 1024 == 0 else 1
    m = s // radix
    ang_c = 2.0 * np.pi * np.outer(np.arange(gc), np.arange(gc)) / gc
    cs = np.concatenate([np.cos(ang_c), -np.sin(ang_c)], axis=1)
    ang_m = 2.0 * np.pi * np.outer(np.arange(m), np.arange(m)) / m
    csm = np.concatenate([np.cos(ang_m), np.sin(ang_m)], axis=1) / math.sqrt(s * gc)
    if radix > 1:
        ang_t = 2.0 * np.pi * np.outer(np.arange(1, radix), np.arange(m)) / s
        tw = np.stack([np.cos(ang_t), np.sin(ang_t)], axis=1)
        tw = np.broadcast_to(tw[..., None], (radix - 1, 2, m, LANES))
    else:
        tw = np.zeros((1, 2, m, LANES))
    return (radix, m, jnp.asarray(cs, BF16), jnp.asarray(csm, BF16), jnp.asarray(tw, F32))


def _fourier_mixer(h, mod, g):
    b, s, d = h.shape
    gc = d // FNET_GROUPS
    radix, m, cs, csm, tw = _fnet_tables(s, gc)
    tm = min(m, 256)
    steps = m // tm
    h_specs = [pl.BlockSpec((1, tm, d), lambda i, j, q=q: (i, q * steps + j, 0)) for q in range(radix)]
    y_spec = pl.BlockSpec((1, radix, tm, d), lambda i, j: (i, 0, j, 0))
    yr, yi = pl.pallas_call(
        functools.partial(_fnet_chan_kernel, gc=gc, radix=radix),
        out_shape=(jax.ShapeDtypeStruct((b, radix, m, d), BF16),) * 2,
        grid=(b, steps),
        in_specs=h_specs + [pl.BlockSpec((1, 6, d), lambda i, j: (i, 0, 0)),
                            _resident((1, d)), _resident((gc, 2 * gc)),
                            pl.BlockSpec((tw.shape[0], 2, tm, LANES), lambda i, j: (0, 0, j, 0))],
        out_specs=(y_spec, y_spec),
        scratch_shapes=[pltpu.VMEM((radix, tm, d), BF16)],
        compiler_params=_params(2),
        name="fnet_chan",
    )(*([h] * radix), mod, g.reshape(1, d), cs, tw)
    tc = 256
    return pl.pallas_call(
        functools.partial(_fnet_pos_kernel, radix=radix, m=m),
        out_shape=jax.ShapeDtypeStruct((b, s, d), BF16),
        grid=(b, d // tc),
        in_specs=[pl.BlockSpec((1, radix, m, tc), lambda i, j: (i, 0, 0, j)),
                  pl.BlockSpec((1, radix, m, tc), lambda i, j: (i, 0, 0, j)),
                  _resident((m, 2 * m))],
        out_specs=pl.BlockSpec((1, s, tc), lambda i, j: (i, 0, j)),
        scratch_shapes=[pltpu.VMEM((tc // LANES, s if radix > 1 else SUBLANES, LANES), F32)],
        compiler_params=_params(2),
        name="fnet_pos",
    )(yr, yi, csm)


def _qkv_kernel(h_ref, mod_ref, g_ref, w_ref, same_ref, hg_ref, cos_ref, sin_ref,
                q_ref, k_ref, v_ref, *, nq, nk, sub):
    t = h_ref.shape[1]

    def project(r0):
        n = _norm_mod(h_ref[0, r0:r0 + sub, :], g_ref[...], mod_ref, 0).astype(BF16)
        return _dot(n, w_ref[...])

    def epilogue(r0, acc):
        qk = acc[:, :nq + nk]
        sq = (qk * qk).astype(BF16)
        width = same_ref.shape[0]
        ssq = jnp.concatenate([_dot(sq[:, c:c + width], same_ref[...]) for c in range(0, nq + nk, width)], axis=1)
        qk = qk * lax.rsqrt(ssq * (1.0 / HEAD_DIM) + NORM_EPS) * hg_ref[...]
        cos = cos_ref[r0:r0 + sub, :]
        sin = sin_ref[r0:r0 + sub, :]
        for c in range((nq + nk) // LANES):
            x = qk[:, c * LANES:(c + 1) * LANES]
            y = (x * cos + pltpu.roll(x, LANES // 2, 1) * sin).astype(BF16)
            if c < nq // LANES:
                q_ref[0, r0:r0 + sub, c * LANES:(c + 1) * LANES] = y
            else:
                k_ref[0, r0:r0 + sub, c * LANES - nq:(c + 1) * LANES - nq] = y
        v_ref[0, r0:r0 + sub, :] = acc[:, nq + nk:].astype(BF16)

    _pipeline_rows(t, sub, project, epilogue)


def _to_block_lanes(x, slots):
    lead = x.shape[:-1]
    x = x.reshape(*lead, -1, slots, 2, 2, ROPE_PAIRS)
    x = jnp.moveaxis(x, -2, -4)
    if slots == 1:
        x = jnp.broadcast_to(x, (*x.shape[:-3], 2, 2, ROPE_PAIRS))
    return x.reshape(*lead, -1)


def _rope_tables(s, with_rope):
    if not with_rope:
        return jnp.ones((s, LANES), F32), jnp.zeros((s, LANES), F32)
    rows = s // GRID_W
    row = jnp.repeat(jnp.arange(rows, dtype=F32), GRID_W)
    col = jnp.tile(jnp.arange(GRID_W, dtype=F32), rows)
    inv = ROPE_THETA ** (-jnp.arange(ROPE_PAIRS, dtype=F32) / ROPE_PAIRS)
    ang_r = row[:, None] * inv[None, :]
    ang_c = col[:, None] * inv[None, :]
    cos = jnp.concatenate([jnp.cos(ang_r), jnp.cos(ang_c)], axis=1)
    sin = jnp.concatenate([jnp.sin(ang_r), jnp.sin(ang_c)], axis=1)
    reps = LANES // (4 * ROPE_PAIRS)
    return jnp.tile(cos, (1, 2 * reps)), jnp.concatenate([jnp.tile(-sin, (1, reps)), jnp.tile(sin, (1, reps))], axis=1)


def _qkv(h, mod, g, w_perm, q_g, k_g, with_rope):
    b, s, d = h.shape
    nq = d
    nk = KV_HEADS * LANES
    t = _wide_row_tile(s)
    width = 2 * LANES
    lane = np.arange(width)
    head = 2 * (lane // LANES) + (lane % HEAD_DIM) // (HEAD_DIM // 2)
    same = (head[:, None] == head[None, :]).astype(np.float32)
    assert (nq + nk) % width == 0
    q_gain = _to_block_lanes(jnp.tile(q_g * (HEAD_DIM ** -0.5 * LOG2E), 2), 2)
    hg = jnp.concatenate([jnp.tile(q_gain, nq // LANES), jnp.tile(_to_block_lanes(k_g, 1), nk // LANES)])
    hg = hg.reshape(1, nq + nk)
    cos, sin = _rope_tables(s, with_rope)
    return pl.pallas_call(
        functools.partial(_qkv_kernel, nq=nq, nk=nk, sub=min(t, 256)),
        out_shape=(jax.ShapeDtypeStruct((b, s, nq), BF16),
                   jax.ShapeDtypeStruct((b, s, nk), BF16),
                   jax.ShapeDtypeStruct((b, s, nk), BF16)),
        grid=(b, s // t),
        in_specs=[pl.BlockSpec((1, t, d), lambda i, j: (i, j, 0)),
                  pl.BlockSpec((1, 6, d), lambda i, j: (i, 0, 0)),
                  _resident((1, d)), _resident(w_perm.shape),
                  _resident((width, width)), _resident((1, nq + nk)),
                  pl.BlockSpec((t, LANES), lambda i, j: (j, 0)),
                  pl.BlockSpec((t, LANES), lambda i, j: (j, 0))],
        out_specs=(pl.BlockSpec((1, t, nq), lambda i, j: (i, j, 0)),
                   pl.BlockSpec((1, t, nk), lambda i, j: (i, j, 0)),
                   pl.BlockSpec((1, t, nk), lambda i, j: (i, j, 0))),
        compiler_params=_params(2),
        name="qkv",
    )(h, mod, g.reshape(1, d), w_perm, jnp.asarray(same, BF16), hg, cos, sin)


def _attn_kernel(sink_ref, q_ref, *refs, band, s_len, n_ctx):
    if band:
        k_ref, v_ref, kc_ref, vc_ref, o_ref = refs
    else:
        kc_ref, vc_ref, o_ref = refs
    nq = ATTN_BLOCK
    n_blocks = q_ref.shape[1] // nq
    n_band = 3 * ATTN_BLOCK if band else 0
    n_keys = n_band + n_ctx
    group = q_ref.shape[2] // (KV_HEADS * HEAD_DIM)
    pairs = group // 2
    lane = lax.broadcasted_iota(jnp.int32, (1, LANES), 1)
    low = lane < HEAD_DIM
    first = (lane % HEAD_DIM) < HEAD_DIM // 2

    bias_band, k_bd, v_bd = [], [], []
    for qb in range(n_blocks):
        j = pl.program_id(1) * n_blocks + qb
        if band:
            start = pl.multiple_of(jnp.clip((j - 1) * ATTN_BLOCK, 0, s_len - n_band), ATTN_BLOCK)
            qpos = j * ATTN_BLOCK + lax.broadcasted_iota(jnp.int32, (nq, n_band), 0)
            kpos = start + lax.broadcasted_iota(jnp.int32, (nq, n_band), 1)
            bias_band.append(jnp.where(jnp.abs(qpos - kpos) <= WINDOW, 0.0, NEG_BIG).astype(F32))
        for kv in range(KV_HEADS):
            lanes_kv = slice(kv * LANES, (kv + 1) * LANES)
            k_all = kc_ref[0, :, lanes_kv]
            v_all = vc_ref[0, :, lanes_kv]
            if band:
                k_all = jnp.concatenate([k_ref[0, pl.ds(start, n_band), lanes_kv], k_all], axis=0)
                v_all = jnp.concatenate([v_ref[0, pl.ds(start, n_band), lanes_kv], v_all], axis=0)
            zero = jnp.zeros_like(k_all)
            k_bd.append(jnp.concatenate([jnp.where(first, k_all, zero), jnp.where(first, zero, k_all)], axis=0))
            v_bd.append(jnp.concatenate([jnp.where(low, v_all, zero), jnp.where(low, zero, v_all)], axis=0))

    def logits(qb, kv, m):
        blk = slice((kv * pairs + m) * LANES, (kv * pairs + m + 1) * LANES)
        return _dot_nt(q_ref[0, qb * nq:(qb + 1) * nq, blk], k_bd[qb * KV_HEADS + kv])

    def softmax(qb, kv, m, sc):
        probs, denoms = [], []
        for half in range(2):
            sh = sc[:, half * n_keys:(half + 1) * n_keys]
            if band:
                sh = jnp.concatenate([sh[:, :n_band] + bias_band[qb], sh[:, n_band:]], axis=1)
            sink = sink_ref[kv * group + 2 * m + half] * LOG2E
            mx = jnp.maximum(jnp.max(sh, axis=1, keepdims=True), sink)
            e = jnp.exp2(sh - mx)
            denoms.append(jnp.sum(e, axis=1, keepdims=True) + jnp.exp2(sink - mx))
            probs.append(e.astype(BF16))
        return jnp.concatenate(probs, axis=1), jnp.where(low, denoms[0], denoms[1])

    def weighted_values(qb, kv, m, p, denom):
        blk = slice((kv * pairs + m) * LANES, (kv * pairs + m + 1) * LANES)
        o_ref[0, qb * nq:(qb + 1) * nq, blk] = (_dot(p, v_bd[qb * KV_HEADS + kv]) / denom).astype(BF16)

    items = [(qb, kv, m) for qb in range(n_blocks) for kv in range(KV_HEADS) for m in range(pairs)]
    sc = logits(*items[0])
    pending = None
    for idx, item in enumerate(items):
        nxt = logits(*items[idx + 1]) if idx + 1 < len(items) else None
        p, denom = softmax(*item, sc)
        if pending is not None:
            weighted_values(*pending)
        pending = (*item, p, denom)
        sc = nxt
    weighted_values(*pending)


def _attention(q, k, v, kc, vc, sink, band):
    b, s, d = q.shape
    n_ctx = kc.shape[1]
    nkv = kc.shape[2]
    tq = 4 * ATTN_BLOCK if s % (4 * ATTN_BLOCK) == 0 else (2 * ATTN_BLOCK if s % (2 * ATTN_BLOCK) == 0 else ATTN_BLOCK)
    in_specs = [pl.BlockSpec(memory_space=pltpu.SMEM),
                pl.BlockSpec((1, tq, d), lambda i, j: (i, j, 0))]
    args = [sink.astype(F32), q]
    if band:
        in_specs += [pl.BlockSpec((1, s, nkv), lambda i, j: (i, 0, 0))] * 2
        args += [k, v]
    in_specs += [pl.BlockSpec((1, n_ctx, nkv), lambda i, j: (i, 0, 0))] * 2
    args += [kc, vc]
    return pl.pallas_call(
        functools.partial(_attn_kernel, band=band, s_len=s, n_ctx=n_ctx),
        out_shape=jax.ShapeDtypeStruct((b, s, d), BF16),
        grid=(b, s // tq),
        in_specs=in_specs,
        out_specs=pl.BlockSpec((1, tq, d), lambda i, j: (i, j, 0)),
        compiler_params=_params(2),
        name="attn_band" if band else "attn_ctx",
    )(*args)


def _permute_qkv_columns(w_qkv, d):
    w = w_qkv.astype(BF16)
    nkv = KV_HEADS * HEAD_DIM
    wv = w[:, d + nkv:].reshape(-1, KV_HEADS, 1, HEAD_DIM)
    wv = jnp.broadcast_to(wv, (w.shape[0], KV_HEADS, LANES // HEAD_DIM, HEAD_DIM)).reshape(w.shape[0], -1)
    return jnp.concatenate([_to_block_lanes(w[:, :d], 2), _to_block_lanes(w[:, d:d + nkv], 1), wv], axis=1)


def _conv_kernel(h_ref, hp_ref, hn_ref, mod_ref, g_ref, w_in_ref, b_in_ref, dw_ref, dwb_ref, lng_ref,
                       lnb_ref, w_ref, b_ref, o_ref, n_ref, ext_ref, y_ref, *, rows):
    j = pl.program_id(1)
    t, d = h_ref.shape[1], h_ref.shape[2]
    n_chunks = d // LANES
    halo = CONV_HALO
    n_ref[0:halo, :] = _norm_mod(hp_ref[0], g_ref[...], mod_ref, 0).astype(BF16)
    n_ref[halo:halo + t, :] = _norm_mod(h_ref[0], g_ref[...], mod_ref, 0).astype(BF16)
    n_ref[halo + t:, :] = _norm_mod(hn_ref[0], g_ref[...], mod_ref, 0).astype(BF16)
    shift = halo - CONV_WIDTH // 2

    def glu(c):
        a = _dot(n_ref[...], w_in_ref[:, 2 * c * LANES:2 * (c + 1) * LANES]) + b_in_ref[:, 2 * c * LANES:2 * (c + 1) * LANES]
        u = a[:, :LANES] * jax.nn.sigmoid(a[:, LANES:])
        ext_ref[c, 0:halo, :] = jnp.where(j > 0, u[0:halo], 0.0)
        ext_ref[c, halo:halo + t, :] = u[halo:halo + t]
        ext_ref[c, halo + t:, :] = jnp.where(j < pl.num_programs(1) - 1, u[halo + t:], 0.0)

    def conv(c):
        for r0 in range(0, t, rows):
            acc = jnp.zeros((rows, LANES), F32)
            for tap in range(CONV_WIDTH):
                acc = acc + dw_ref[c, tap:tap + 1, :] * ext_ref[c, r0 + tap + shift:r0 + tap + shift + rows, :]
            y_ref[c, r0:r0 + rows, :] = acc

    glu(0)
    for c in range(n_chunks):
        if c + 1 < n_chunks:
            glu(c + 1)
        conv(c)
    y = jnp.concatenate([y_ref[c] for c in range(n_chunks)], axis=1)
    y = _layer_norm(y + dwb_ref[...], lng_ref[...], lnb_ref[...])
    y = (y * jax.nn.sigmoid(y)).astype(BF16)
    acc = _dot(y, w_ref[...]) + b_ref[...]
    o_ref[0] = h_ref[0] + mod_ref[0, 2:3, :] * acc


def _conv_module(h, mod, g, w_in, b_in, dw, dw_b, ln_g, ln_b, w_out, b_out):
    b, s, d = h.shape
    t = _wide_row_tile(s)
    hb = t // CONV_HALO
    last = s // CONV_HALO - 1
    n_chunks = d // LANES
    taps_pad = -(-CONV_WIDTH // SUBLANES) * SUBLANES
    dw_pad = jnp.concatenate([dw, jnp.zeros((taps_pad - CONV_WIDTH, d), F32)], axis=0)
    dw_pad = dw_pad.reshape(taps_pad, n_chunks, LANES).transpose(1, 0, 2)

    def chunk_major(x):
        lead = x.shape[:-1]
        return x.reshape(*lead, 2, n_chunks, LANES).swapaxes(-2, -3).reshape(*lead, 2 * d)

    return pl.pallas_call(
        functools.partial(_conv_kernel, rows=64),
        out_shape=jax.ShapeDtypeStruct(h.shape, F32),
        grid=(b, s // t),
        in_specs=[pl.BlockSpec((1, t, d), lambda i, j: (i, j, 0)),
                  pl.BlockSpec((1, CONV_HALO, d), lambda i, j: (i, jnp.maximum(j * hb - 1, 0), 0)),
                  pl.BlockSpec((1, CONV_HALO, d), lambda i, j: (i, jnp.minimum((j + 1) * hb, last), 0)),
                  pl.BlockSpec((1, 6, d), lambda i, j: (i, 0, 0)),
                  _resident((1, d)), _resident((d, 2 * d)), _resident((1, 2 * d)),
                  _resident((n_chunks, taps_pad, LANES)), _resident((1, d)), _resident((1, d)), _resident((1, d)),
                  _resident((d, d)), _resident((1, d))],
        out_specs=pl.BlockSpec((1, t, d), lambda i, j: (i, j, 0)),
        scratch_shapes=[pltpu.VMEM((t + 2 * CONV_HALO, d), BF16),
                        pltpu.VMEM((n_chunks, t + 2 * CONV_HALO, LANES), F32),
                        pltpu.VMEM((n_chunks, t, LANES), F32)],
        compiler_params=_params(2),
        name="conv",
    )(h, h, h, mod, g.reshape(1, d), chunk_major(w_in.astype(BF16)), chunk_major(b_in.reshape(1, 2 * d)),
      dw_pad, dw_b.reshape(1, d), ln_g.reshape(1, d), ln_b.reshape(1, d), w_out.astype(BF16), b_out.reshape(1, d))


def _gmlp_kernel(h_ref, mod_ref, g_ref, w_in_ref, b_in_ref, lng_ref, lnb_ref, ws_ref, bs_ref, w_out_ref,
                 o_ref, gate_ref, *, sub):
    t, d = h_ref.shape[1], h_ref.shape[2]
    gd = d // GMLP_GROUPS

    def project(r0):
        n = _norm_mod(h_ref[0, r0:r0 + sub, :], g_ref[...], mod_ref, 0).astype(BF16)
        return _dot(n, w_in_ref[...])

    def gate(r0, acc):
        z = jax.nn.gelu(acc + b_in_ref[...])
        u = z[:, :d]
        v = _layer_norm(z[:, d:], lng_ref[...], lnb_ref[...]).astype(BF16)
        for c in range(sub // GMLP_CHUNK):
            rs = slice(c * GMLP_CHUNK, (c + 1) * GMLP_CHUNK)
            for gi in range(GMLP_GROUPS):
                ls = slice(gi * gd, (gi + 1) * gd)
                sg = _dot(ws_ref[gi], v[rs, ls]) + jnp.tile(bs_ref[gi], (1, gd // LANES))
                gate_ref[r0 + c * GMLP_CHUNK:r0 + (c + 1) * GMLP_CHUNK, ls] = (u[rs, ls] * sg).astype(BF16)

    def project_out(r0):
        y = _dot(gate_ref[r0:r0 + sub, :], w_out_ref[...])
        o_ref[0, r0:r0 + sub, :] = h_ref[0, r0:r0 + sub, :] + mod_ref[0, 2:3, :] * y

    acc = project(0)
    for r0 in range(0, t, sub):
        nxt = project(r0 + sub) if r0 + sub < t else None
        gate(r0, acc)
        if r0 > 0:
            project_out(r0 - sub)
        acc = nxt
    project_out(t - sub)


def _chunk_gmlp(h, mod, g, w_in, b_in, ln_g, ln_b, w_s, b_s, w_out):
    b, s, d = h.shape
    t = 1024 if s % 1024 == 0 else (256 if s % 256 == 0 else GMLP_CHUNK)
    bs = jnp.broadcast_to(b_s[:, :, None], (GMLP_GROUPS, GMLP_CHUNK, LANES)).astype(F32)
    return pl.pallas_call(
        functools.partial(_gmlp_kernel, sub=min(t, 256)),
        out_shape=jax.ShapeDtypeStruct(h.shape, F32),
        grid=(b, s // t),
        in_specs=[pl.BlockSpec((1, t, d), lambda i, j: (i, j, 0)),
                  pl.BlockSpec((1, 6, d), lambda i, j: (i, 0, 0)),
                  _resident((1, d)), _resident((d, 2 * d)), _resident((1, 2 * d)),
                  _resident((1, d)), _resident((1, d)),
                  _resident((GMLP_GROUPS, GMLP_CHUNK, GMLP_CHUNK)), _resident((GMLP_GROUPS, GMLP_CHUNK, LANES)),
                  _resident((d, d))],
        out_specs=pl.BlockSpec((1, t, d), lambda i, j: (i, j, 0)),
        scratch_shapes=[pltpu.VMEM((t, d), BF16)],
        compiler_params=_params(2),
        name="gmlp",
    )(h, mod, g.reshape(1, d), w_in.astype(BF16), b_in.reshape(1, 2 * d), ln_g.reshape(1, d),
      ln_b.reshape(1, d), w_s.astype(BF16), bs, w_out.astype(BF16))


def kernel(x, c, ctx, c_ctx, ada_w, ada_b, norm1_g, norm2_g, mlp_w1, mlp_w2, fnet_w, fnet_b, attn_w_qkv, attn_q_g, attn_k_g, attn_sink, attn_w_o, conv_w_in, conv_b_in, conv_dw, conv_dw_b, conv_ln_g, conv_ln_b, conv_w_out, conv_b_out, gmlp_w_in, gmlp_b_in, gmlp_ln_g, gmlp_ln_b, gmlp_w_s, gmlp_b_s, gmlp_w_out):
    bsz, _, d = x.shape
    depth = ada_w.shape[0]
    n_rows = -(-(bsz + 1) // 8) * 8
    c_rows = jnp.concatenate([c, c_ctx[None, :], jnp.zeros((n_rows - bsz - 1, d), F32)], axis=0)
    mods = _adaln(c_rows, ada_w, ada_b).reshape(depth, n_rows, 6, d)

    w1_all = mlp_w1.astype(BF16)
    w2_all = mlp_w2.astype(BF16)
    h_lat, h_ctx = x, ctx
    for i in range(depth):
        last = i == depth - 1
        kind, li = i % N_MIXERS, i // N_MIXERS
        mod_lat = mods[i, :bsz]
        mod_ctx = jnp.broadcast_to(mods[i, bsz], (bsz, 6, d))
        streams = [(h_lat, mod_lat)] + ([] if last else [(h_ctx, mod_ctx)])
        mixed = []
        if kind == 1:
            w_perm = _permute_qkv_columns(attn_w_qkv[li], d)
            w_o = attn_w_o[li].astype(BF16)
            zero_b = jnp.zeros((d,), F32)
            q, k, v = _qkv(h_lat, mod_lat, norm1_g[i], w_perm, attn_q_g[li], attn_k_g[li], True)
            qc, kc, vc = _qkv(h_ctx, mod_ctx, norm1_g[i], w_perm, attn_q_g[li], attn_k_g[li], False)
            mixed.append((h_lat, (_attention(q, k, v, kc, vc, attn_sink[li], True), w_o, zero_b)))
            if not last:
                mixed.append((h_ctx, (_attention(qc, None, None, kc, vc, attn_sink[li], False), w_o, zero_b)))
        else:
            for h, mod in streams:
                if kind == 0:
                    mixed.append((h, (_fourier_mixer(h, mod, norm1_g[i]), fnet_w[li].astype(BF16), fnet_b[li])))
                elif kind == 2:
                    mixed.append((_conv_module(h, mod, norm1_g[i], conv_w_in[li], conv_b_in[li], conv_dw[li],
                                               conv_dw_b[li], conv_ln_g[li], conv_ln_b[li], conv_w_out[li],
                                               conv_b_out[li]), None))
                else:
                    mixed.append((_chunk_gmlp(h, mod, norm1_g[i], gmlp_w_in[li], gmlp_b_in[li], gmlp_ln_g[li],
                                              gmlp_ln_b[li], gmlp_w_s[li], gmlp_b_s[li], gmlp_w_out[li]), None))
        h_lat = _mlp(mixed[0][0], mod_lat, norm2_g[i], w1_all, w2_all, i, mixed[0][1])
        if not last:
            h_ctx = _mlp(mixed[1][0], mod_ctx, norm2_g[i], w1_all, w2_all, i, mixed[1][1])
    return h_lat
```

```python
import functools
import math

import numpy as np
import jax
import jax.numpy as jnp
from jax import lax
from jax.experimental import pallas as pl
from jax.experimental.pallas import tpu as pltpu

F32 = jnp.float32
BF16 = jnp.bfloat16

NORM_EPS = 1e-6
GRID_W = 64
FNET_GROUPS = 4
HEAD_DIM = 64
KV_HEADS = 2
WINDOW = 128
ATTN_BLOCK = 128
ROPE_THETA = 10000.0
ROPE_PAIRS = HEAD_DIM // 4
CONV_WIDTH = 31
CONV_HALO = 16
GMLP_CHUNK = 128
GMLP_GROUPS = 4
N_MIXERS = 4
LANES = 128
SUBLANES = 8
VMEM_LIMIT = 56 * 1024 * 1024
NEG_BIG = -1e30
LOG2E = math.log2(math.e)


def _resident(shape):
    nd = len(shape)
    return pl.BlockSpec(shape, lambda *_: (0,) * nd, pipeline_mode=pl.Buffered(1))


def _params(n_grid):
    return pltpu.CompilerParams(dimension_semantics=("parallel",) * n_grid, vmem_limit_bytes=VMEM_LIMIT)


def _row_tile(s):
    return 512 if s % 512 == 0 else s


def _wide_row_tile(s):
    return 1024 if s % 1024 == 0 else _row_tile(s)


def _dot(a, b):
    return jnp.dot(a, b, preferred_element_type=F32)


def _dot_nt(a, b):
    return lax.dot_general(a, b, (((1,), (1,)), ((), ())), preferred_element_type=F32)


def _norm_mod(x, g, mod_ref, row):
    ms = jnp.mean(x * x, axis=-1, keepdims=True)
    gain = g * (1.0 + mod_ref[0, row + 1:row + 2, :])
    return x * lax.rsqrt(ms + NORM_EPS) * gain + mod_ref[0, row:row + 1, :]


def _pipeline_rows(t, sub, project, finish):
    acc = project(0)
    for r0 in range(0, t, sub):
        nxt = project(r0 + sub) if r0 + sub < t else None
        finish(r0, acc)
        acc = nxt


def _layer_norm(x, g, b):
    mu = jnp.mean(x, axis=-1, keepdims=True)
    xc = x - mu
    var = jnp.mean(xc * xc, axis=-1, keepdims=True)
    return xc * lax.rsqrt(var + NORM_EPS) * g + b


def _adaln_kernel(c_ref, w_ref, b_ref, o_ref):
    c = c_ref[...]
    s = (c * jax.nn.sigmoid(c)).astype(BF16)
    o_ref[0] = _dot(s, w_ref[0].astype(BF16)) + b_ref[0]


def _adaln(c_rows, ada_w, ada_b):
    depth, d, n = ada_w.shape
    r = c_rows.shape[0]
    tn = 1024
    return pl.pallas_call(
        _adaln_kernel,
        out_shape=jax.ShapeDtypeStruct((depth, r, n), F32),
        grid=(depth, n // tn),
        in_specs=[pl.BlockSpec((r, d), lambda l, j: (0, 0)),
                  pl.BlockSpec((1, d, tn), lambda l, j: (l, 0, j)),
                  pl.BlockSpec((1, 1, tn), lambda l, j: (l, 0, j))],
        out_specs=pl.BlockSpec((1, r, tn), lambda l, j: (l, 0, j)),
        compiler_params=_params(2),
        name="adaln",
    )(c_rows, ada_w, ada_b.reshape(depth, 1, n))


def _mlp_kernel(h_ref, mod_ref, g_ref, w1_ref, w2_ref, *refs, ff_chunk):
    x = h_ref[0]
    if len(refs) == 4:
        y_ref, wo_ref, bo_ref, o_ref = refs
        x = x + mod_ref[0, 2:3, :] * (_dot(y_ref[0], wo_ref[...]) + bo_ref[...])
    else:
        o_ref, = refs
    n = _norm_mod(x, g_ref[...], mod_ref, 3).astype(BF16)
    d_ff = w1_ref.shape[2]
    acc = jnp.zeros(x.shape, F32)
    for c in range(d_ff // ff_chunk):
        a = _dot(n, w1_ref[0, :, c * ff_chunk:(c + 1) * ff_chunk])
        a = jnp.square(jnp.maximum(a, 0.0)).astype(BF16)
        acc = acc + _dot(a, w2_ref[0, c * ff_chunk:(c + 1) * ff_chunk, :])
    o_ref[0] = x + mod_ref[0, 5:6, :] * acc


def _mlp(h, mod, g, w1, w2, layer, pending=None):
    b, s, d = h.shape
    t = _wide_row_tile(s)
    d_ff = w1.shape[2]
    in_specs = [pl.BlockSpec((1, t, d), lambda i, j: (i, j, 0)),
                pl.BlockSpec((1, 6, d), lambda i, j: (i, 0, 0)),
                _resident((1, d)),
                pl.BlockSpec((1, d, d_ff), lambda i, j: (layer, 0, 0), pipeline_mode=pl.Buffered(1)),
                pl.BlockSpec((1, d_ff, d), lambda i, j: (layer, 0, 0), pipeline_mode=pl.Buffered(1))]
    args = [h, mod, g.reshape(1, d), w1, w2]
    if pending is not None:
        y, wo, bo = pending
        k = y.shape[-1]
        in_specs += [pl.BlockSpec((1, t, k), lambda i, j: (i, j, 0)), _resident((k, d)), _resident((1, d))]
        args += [y, wo, bo.reshape(1, d)]
    return pl.pallas_call(
        functools.partial(_mlp_kernel, ff_chunk=1024),
        out_shape=jax.ShapeDtypeStruct(h.shape, F32),
        grid=(b, s // t),
        in_specs=in_specs,
        out_specs=pl.BlockSpec((1, t, d), lambda i, j: (i, j, 0)),
        compiler_params=_params(2),
        name="mlp",
    )(*args)


def _scaled_sum(ca, a, cb, b):
    mag = abs(ca)
    if ca > 0 and cb > 0:
        return mag * (a + b)
    if ca > 0:
        return mag * (a - b)
    if cb > 0:
        return mag * (b - a)
    return (-mag) * (a + b)


def _small_dft(z):
    r = len(z)
    if r == 1:
        return z
    even = _small_dft(z[0::2])
    odd = _small_dft(z[1::2])
    out = [None] * r
    for k in range(r // 2):
        (er, ei), (a, b) = even[k], odd[k]
        if k == 0:
            tr, ti = a, b
        elif 4 * k == r:
            out[k] = (er + b, ei - a)
            out[k + r // 2] = (er - b, ei + a)
            continue
        else:
            c, s = math.cos(2 * math.pi * k / r), math.sin(2 * math.pi * k / r)
            if abs(abs(c) - abs(s)) < 1e-12:
                tr, ti = _scaled_sum(c, a, s, b), _scaled_sum(c, b, -s, a)
            else:
                tr, ti = c * a + s * b, c * b - s * a
        out[k] = (er + tr, ei + ti)
        out[k + r // 2] = (er - tr, ei - ti)
    return out


def _fnet_chan_kernel(*refs, gc, radix):
    h_refs = refs[:radix]
    mod_ref, g_ref, cs_ref, tw_ref, yr_ref, yi_ref, n_ref = refs[radix:]
    d = h_refs[0].shape[2]
    for q in range(radix):
        n_ref[q] = _norm_mod(h_refs[q][0], g_ref[...], mod_ref, 0).astype(BF16)
    for gi in range(d // gc):
        ls = slice(gi * gc, (gi + 1) * gc)
        z = []
        for q in range(radix):
            r = _dot(n_ref[q, :, ls], cs_ref[...])
            z.append((r[:, :gc], r[:, gc:]))
        for p, (br, bi) in enumerate(_small_dft(z)):
            if p > 0:
                c = jnp.tile(tw_ref[p - 1, 0], (1, gc // LANES))
                s = jnp.tile(tw_ref[p - 1, 1], (1, gc // LANES))
                br, bi = c * br + s * bi, c * bi - s * br
            yr_ref[0, p, :, ls] = br.astype(BF16)
            yi_ref[0, p, :, ls] = bi.astype(BF16)


def _fnet_pos_kernel(yr_ref, yi_ref, csm_ref, o_ref, x_ref, *, radix, m):
    tc = yr_ref.shape[3]
    for p in range(radix):
        x = _dot(csm_ref[...], jnp.concatenate([yr_ref[0, p], yi_ref[0, p]], axis=0))
        if radix == 1:
            o_ref[0] = x.astype(BF16)
        else:
            for cb in range(tc // LANES):
                x_ref[cb, pl.ds(p, m, stride=radix), :] = x[:, cb * LANES:(cb + 1) * LANES]
    if radix > 1:
        for cb in range(tc // LANES):
            o_ref[0, :, cb * LANES:(cb + 1) * LANES] = x_ref[cb].astype(BF16)


def _fnet_tables(s, gc):
    radix = 8 if s % 1024 == 0 else 1
    m = s // radix
    ang_c = 2.0 * np.pi * np.outer(np.arange(gc), np.arange(gc)) / gc
    cs = np.concatenate([np.cos(ang_c), -np.sin(ang_c)], axis=1)
    ang_m = 2.0 * np.pi * np.outer(np.arange(m), np.arange(m)) / m
    csm = np.concatenate([np.cos(ang_m), np.sin(ang_m)], axis=1) / math.sqrt(s * gc)
    if radix > 1:
        ang_t = 2.0 * np.pi * np.outer(np.arange(1, radix), np.arange(m)) / s
        tw = np.stack([np.cos(ang_t), np.sin(ang_t)], axis=1)
        tw = np.broadcast_to(tw[..., None], (radix - 1, 2, m, LANES))
    else:
        tw = np.zeros((1, 2, m, LANES))
    return (radix, m, jnp.asarray(cs, BF16), jnp.asarray(csm, BF16), jnp.asarray(tw, F32))


def _fourier_mixer(h, mod, g):
    b, s, d = h.shape
    gc = d // FNET_GROUPS
    radix, m, cs, csm, tw = _fnet_tables(s, gc)
    tm = min(m, 256)
    steps = m // tm
    h_specs = [pl.BlockSpec((1, tm, d), lambda i, j, q=q: (i, q * steps + j, 0)) for q in range(radix)]
    y_spec = pl.BlockSpec((1, radix, tm, d), lambda i, j: (i, 0, j, 0))
    yr, yi = pl.pallas_call(
        functools.partial(_fnet_chan_kernel, gc=gc, radix=radix),
        out_shape=(jax.ShapeDtypeStruct((b, radix, m, d), BF16),) * 2,
        grid=(b, steps),
        in_specs=h_specs + [pl.BlockSpec((1, 6, d), lambda i, j: (i, 0, 0)),
                            _resident((1, d)), _resident((gc, 2 * gc)),
                            pl.BlockSpec((tw.shape[0], 2, tm, LANES), lambda i, j: (0, 0, j, 0))],
        out_specs=(y_spec, y_spec),
        scratch_shapes=[pltpu.VMEM((radix, tm, d), BF16)],
        compiler_params=_params(2),
        name="fnet_chan",
    )(*([h] * radix), mod, g.reshape(1, d), cs, tw)
    tc = 256
    return pl.pallas_call(
        functools.partial(_fnet_pos_kernel, radix=radix, m=m),
        out_shape=jax.ShapeDtypeStruct((b, s, d), BF16),
        grid=(b, d // tc),
        in_specs=[pl.BlockSpec((1, radix, m, tc), lambda i, j: (i, 0, 0, j)),
                  pl.BlockSpec((1, radix, m, tc), lambda i, j: (i, 0, 0, j)),
                  _resident((m, 2 * m))],
        out_specs=pl.BlockSpec((1, s, tc), lambda i, j: (i, 0, j)),
        scratch_shapes=[pltpu.VMEM((tc // LANES, s if radix > 1 else SUBLANES, LANES), F32)],
        compiler_params=_params(2),
        name="fnet_pos",
    )(yr, yi, csm)


def _qkv_kernel(h_ref, mod_ref, g_ref, w_ref, same_ref, hg_ref, cos_ref, sin_ref,
                q_ref, k_ref, v_ref, *, nq, nk, sub):
    t = h_ref.shape[1]

    def project(r0):
        n = _norm_mod(h_ref[0, r0:r0 + sub, :], g_ref[...], mod_ref, 0).astype(BF16)
        return _dot(n, w_ref[...])

    def epilogue(r0, acc):
        qk = acc[:, :nq + nk]
        sq = (qk * qk).astype(BF16)
        width = same_ref.shape[0]
        ssq = jnp.concatenate([_dot(sq[:, c:c + width], same_ref[...]) for c in range(0, nq + nk, width)], axis=1)
        qk = qk * lax.rsqrt(ssq * (1.0 / HEAD_DIM) + NORM_EPS) * hg_ref[...]
        cos = cos_ref[r0:r0 + sub, :]
        sin = sin_ref[r0:r0 + sub, :]
        for c in range((nq + nk) // LANES):
            x = qk[:, c * LANES:(c + 1) * LANES]
            y = (x * cos + pltpu.roll(x, LANES // 2, 1) * sin).astype(BF16)
            if c < nq // LANES:
                q_ref[0, r0:r0 + sub, c * LANES:(c + 1) * LANES] = y
            else:
                k_ref[0, r0:r0 + sub, c * LANES - nq:(c + 1) * LANES - nq] = y
        v_ref[0, r0:r0 + sub, :] = acc[:, nq + nk:].astype(BF16)

    _pipeline_rows(t, sub, project, epilogue)


def _to_block_lanes(x, slots):
    lead = x.shape[:-1]
    x = x.reshape(*lead, -1, slots, 2, 2, ROPE_PAIRS)
    x = jnp.moveaxis(x, -2, -4)
    if slots == 1:
        x = jnp.broadcast_to(x, (*x.shape[:-3], 2, 2, ROPE_PAIRS))
    return x.reshape(*lead, -1)


def _rope_tables(s, with_rope):
    if not with_rope:
        return jnp.ones((s, LANES), F32), jnp.zeros((s, LANES), F32)
    rows = s // GRID_W
    row = jnp.repeat(jnp.arange(rows, dtype=F32), GRID_W)
    col = jnp.tile(jnp.arange(GRID_W, dtype=F32), rows)
    inv = ROPE_THETA ** (-jnp.arange(ROPE_PAIRS, dtype=F32) / ROPE_PAIRS)
    ang_r = row[:, None] * inv[None, :]
    ang_c = col[:, None] * inv[None, :]
    cos = jnp.concatenate([jnp.cos(ang_r), jnp.cos(ang_c)], axis=1)
    sin = jnp.concatenate([jnp.sin(ang_r), jnp.sin(ang_c)], axis=1)
    reps = LANES // (4 * ROPE_PAIRS)
    return jnp.tile(cos, (1, 2 * reps)), jnp.concatenate([jnp.tile(-sin, (1, reps)), jnp.tile(sin, (1, reps))], axis=1)


def _qkv(h, mod, g, w_perm, q_g, k_g, with_rope):
    b, s, d = h.shape
    nq = d
    nk = KV_HEADS * LANES
    t = _wide_row_tile(s)
    width = 2 * LANES
    lane = np.arange(width)
    head = 2 * (lane // LANES) + (lane % HEAD_DIM) // (HEAD_DIM // 2)
    same = (head[:, None] == head[None, :]).astype(np.float32)
    assert (nq + nk) % width == 0
    q_gain = _to_block_lanes(jnp.tile(q_g * (HEAD_DIM ** -0.5 * LOG2E), 2), 2)
    hg = jnp.concatenate([jnp.tile(q_gain, nq // LANES), jnp.tile(_to_block_lanes(k_g, 1), nk // LANES)])
    hg = hg.reshape(1, nq + nk)
    cos, sin = _rope_tables(s, with_rope)
    return pl.pallas_call(
        functools.partial(_qkv_kernel, nq=nq, nk=nk, sub=min(t, 256)),
        out_shape=(jax.ShapeDtypeStruct((b, s, nq), BF16),
                   jax.ShapeDtypeStruct((b, s, nk), BF16),
                   jax.ShapeDtypeStruct((b, s, nk), BF16)),
        grid=(b, s // t),
        in_specs=[pl.BlockSpec((1, t, d), lambda i, j: (i, j, 0)),
                  pl.BlockSpec((1, 6, d), lambda i, j: (i, 0, 0)),
                  _resident((1, d)), _resident(w_perm.shape),
                  _resident((width, width)), _resident((1, nq + nk)),
                  pl.BlockSpec((t, LANES), lambda i, j: (j, 0)),
                  pl.BlockSpec((t, LANES), lambda i, j: (j, 0))],
        out_specs=(pl.BlockSpec((1, t, nq), lambda i, j: (i, j, 0)),
                   pl.BlockSpec((1, t, nk), lambda i, j: (i, j, 0)),
                   pl.BlockSpec((1, t, nk), lambda i, j: (i, j, 0))),
        compiler_params=_params(2),
        name="qkv",
    )(h, mod, g.reshape(1, d), w_perm, jnp.asarray(same, BF16), hg, cos, sin)


def _attn_kernel(sink_ref, q_ref, *refs, band, s_len, n_ctx):
    if band:
        k_ref, v_ref, kc_ref, vc_ref, o_ref = refs
    else:
        kc_ref, vc_ref, o_ref = refs
    nq = ATTN_BLOCK
    n_blocks = q_ref.shape[1] // nq
    n_band = 3 * ATTN_BLOCK if band else 0
    n_keys = n_band + n_ctx
    group = q_ref.shape[2] // (KV_HEADS * HEAD_DIM)
    pairs = group // 2
    lane = lax.broadcasted_iota(jnp.int32, (1, LANES), 1)
    low = lane < HEAD_DIM
    first = (lane % HEAD_DIM) < HEAD_DIM // 2

    bias_band, k_bd, v_bd = [], [], []
    for qb in range(n_blocks):
        j = pl.program_id(1) * n_blocks + qb
        if band:
            start = pl.multiple_of(jnp.clip((j - 1) * ATTN_BLOCK, 0, s_len - n_band), ATTN_BLOCK)
            qpos = j * ATTN_BLOCK + lax.broadcasted_iota(jnp.int32, (nq, n_band), 0)
            kpos = start + lax.broadcasted_iota(jnp.int32, (nq, n_band), 1)
            bias_band.append(jnp.where(jnp.abs(qpos - kpos) <= WINDOW, 0.0, NEG_BIG).astype(F32))
        for kv in range(KV_HEADS):
            lanes_kv = slice(kv * LANES, (kv + 1) * LANES)
            k_all = kc_ref[0, :, lanes_kv]
            v_all = vc_ref[0, :, lanes_kv]
            if band:
                k_all = jnp.concatenate([k_ref[0, pl.ds(start, n_band), lanes_kv], k_all], axis=0)
                v_all = jnp.concatenate([v_ref[0, pl.ds(start, n_band), lanes_kv], v_all], axis=0)
            zero = jnp.zeros_like(k_all)
            k_bd.append(jnp.concatenate([jnp.where(first, k_all, zero), jnp.where(first, zero, k_all)], axis=0))
            v_bd.append(jnp.concatenate([jnp.where(low, v_all, zero), jnp.where(low, zero, v_all)], axis=0))

    def logits(qb, kv, m):
        blk = slice((kv * pairs + m) * LANES, (kv * pairs + m + 1) * LANES)
        return _dot_nt(q_ref[0, qb * nq:(qb + 1) * nq, blk], k_bd[qb * KV_HEADS + kv])

    def softmax(qb, kv, m, sc):
        probs, denoms = [], []
        for half in range(2):
            sh = sc[:, half * n_keys:(half + 1) * n_keys]
            if band:
                sh = jnp.concatenate([sh[:, :n_band] + bias_band[qb], sh[:, n_band:]], axis=1)
            sink = sink_ref[kv * group + 2 * m + half] * LOG2E
            mx = jnp.maximum(jnp.max(sh, axis=1, keepdims=True), sink)
            e = jnp.exp2(sh - mx)
            denoms.append(jnp.sum(e, axis=1, keepdims=True) + jnp.exp2(sink - mx))
            probs.append(e.astype(BF16))
        return jnp.concatenate(probs, axis=1), jnp.where(low, denoms[0], denoms[1])

    def weighted_values(qb, kv, m, p, denom):
        blk = slice((kv * pairs + m) * LANES, (kv * pairs + m + 1) * LANES)
        o_ref[0, qb * nq:(qb + 1) * nq, blk] = (_dot(p, v_bd[qb * KV_HEADS + kv]) / denom).astype(BF16)

    items = [(qb, kv, m) for qb in range(n_blocks) for kv in range(KV_HEADS) for m in range(pairs)]
    sc = logits(*items[0])
    pending = None
    for idx, item in enumerate(items):
        nxt = logits(*items[idx + 1]) if idx + 1 < len(items) else None
        p, denom = softmax(*item, sc)
        if pending is not None:
            weighted_values(*pending)
        pending = (*item, p, denom)
        sc = nxt
    weighted_values(*pending)


def _attention(q, k, v, kc, vc, sink, band):
    b, s, d = q.shape
    n_ctx = kc.shape[1]
    nkv = kc.shape[2]
    tq = next(n * ATTN_BLOCK for n in (8, 4, 2, 1) if s % (n * ATTN_BLOCK) == 0)
    in_specs = [pl.BlockSpec(memory_space=pltpu.SMEM),
                pl.BlockSpec((1, tq, d), lambda i, j: (i, j, 0))]
    args = [sink.astype(F32), q]
    if band:
        in_specs += [pl.BlockSpec((1, s, nkv), lambda i, j: (i, 0, 0))] * 2
        args += [k, v]
    in_specs += [pl.BlockSpec((1, n_ctx, nkv), lambda i, j: (i, 0, 0))] * 2
    args += [kc, vc]
    return pl.pallas_call(
        functools.partial(_attn_kernel, band=band, s_len=s, n_ctx=n_ctx),
        out_shape=jax.ShapeDtypeStruct((b, s, d), BF16),
        grid=(b, s // tq),
        in_specs=in_specs,
        out_specs=pl.BlockSpec((1, tq, d), lambda i, j: (i, j, 0)),
        compiler_params=_params(2),
        name="attn_band" if band else "attn_ctx",
    )(*args)


def _permute_qkv_columns(w_qkv, d):
    w = w_qkv.astype(BF16)
    nkv = KV_HEADS * HEAD_DIM
    wv = w[:, d + nkv:].reshape(-1, KV_HEADS, 1, HEAD_DIM)
    wv = jnp.broadcast_to(wv, (w.shape[0], KV_HEADS, LANES // HEAD_DIM, HEAD_DIM)).reshape(w.shape[0], -1)
    return jnp.concatenate([_to_block_lanes(w[:, :d], 2), _to_block_lanes(w[:, d:d + nkv], 1), wv], axis=1)


def _conv_kernel(h_ref, hp_ref, hn_ref, mod_ref, g_ref, w_in_ref, b_in_ref, dw_ref, dwb_ref, lng_ref,
                       lnb_ref, w_ref, b_ref, o_ref, n_ref, ext_ref, y_ref, *, rows):
    j = pl.program_id(1)
    t, d = h_ref.shape[1], h_ref.shape[2]
    n_chunks = d // LANES
    halo = CONV_HALO
    n_ref[0:halo, :] = _norm_mod(hp_ref[0], g_ref[...], mod_ref, 0).astype(BF16)
    n_ref[halo:halo + t, :] = _norm_mod(h_ref[0], g_ref[...], mod_ref, 0).astype(BF16)
    n_ref[halo + t:, :] = _norm_mod(hn_ref[0], g_ref[...], mod_ref, 0).astype(BF16)
    shift = halo - CONV_WIDTH // 2

    def glu(c):
        a = _dot(n_ref[...], w_in_ref[:, 2 * c * LANES:2 * (c + 1) * LANES]) + b_in_ref[:, 2 * c * LANES:2 * (c + 1) * LANES]
        u = a[:, :LANES] * jax.nn.sigmoid(a[:, LANES:])
        ext_ref[c, 0:halo, :] = jnp.where(j > 0, u[0:halo], 0.0)
        ext_ref[c, halo:halo + t, :] = u[halo:halo + t]
        ext_ref[c, halo + t:, :] = jnp.where(j < pl.num_programs(1) - 1, u[halo + t:], 0.0)

    def conv(c):
        for r0 in range(0, t, rows):
            acc = jnp.zeros((rows, LANES), F32)
            for tap in range(CONV_WIDTH):
                acc = acc + dw_ref[c, tap:tap + 1, :] * ext_ref[c, r0 + tap + shift:r0 + tap + shift + rows, :]
            y_ref[c, r0:r0 + rows, :] = acc

    glu(0)
    for c in range(n_chunks):
        if c + 1 < n_chunks:
            glu(c + 1)
        conv(c)
    y = jnp.concatenate([y_ref[c] for c in range(n_chunks)], axis=1)
    y = _layer_norm(y + dwb_ref[...], lng_ref[...], lnb_ref[...])
    y = (y * jax.nn.sigmoid(y)).astype(BF16)
    acc = _dot(y, w_ref[...]) + b_ref[...]
    o_ref[0] = h_ref[0] + mod_ref[0, 2:3, :] * acc


def _conv_module(h, mod, g, w_in, b_in, dw, dw_b, ln_g, ln_b, w_out, b_out):
    b, s, d = h.shape
    t = _wide_row_tile(s)
    hb = t // CONV_HALO
    last = s // CONV_HALO - 1
    n_chunks = d // LANES
    taps_pad = -(-CONV_WIDTH // SUBLANES) * SUBLANES
    dw_pad = jnp.concatenate([dw, jnp.zeros((taps_pad - CONV_WIDTH, d), F32)], axis=0)
    dw_pad = dw_pad.reshape(taps_pad, n_chunks, LANES).transpose(1, 0, 2)

    def chunk_major(x):
        lead = x.shape[:-1]
        return x.reshape(*lead, 2, n_chunks, LANES).swapaxes(-2, -3).reshape(*lead, 2 * d)

    return pl.pallas_call(
        functools.partial(_conv_kernel, rows=64),
        out_shape=jax.ShapeDtypeStruct(h.shape, F32),
        grid=(b, s // t),
        in_specs=[pl.BlockSpec((1, t, d), lambda i, j: (i, j, 0)),
                  pl.BlockSpec((1, CONV_HALO, d), lambda i, j: (i, jnp.maximum(j * hb - 1, 0), 0)),
                  pl.BlockSpec((1, CONV_HALO, d), lambda i, j: (i, jnp.minimum((j + 1) * hb, last), 0)),
                  pl.BlockSpec((1, 6, d), lambda i, j: (i, 0, 0)),
                  _resident((1, d)), _resident((d, 2 * d)), _resident((1, 2 * d)),
                  _resident((n_chunks, taps_pad, LANES)), _resident((1, d)), _resident((1, d)), _resident((1, d)),
                  _resident((d, d)), _resident((1, d))],
        out_specs=pl.BlockSpec((1, t, d), lambda i, j: (i, j, 0)),
        scratch_shapes=[pltpu.VMEM((t + 2 * CONV_HALO, d), BF16),
                        pltpu.VMEM((n_chunks, t + 2 * CONV_HALO, LANES), F32),
                        pltpu.VMEM((n_chunks, t, LANES), F32)],
        compiler_params=_params(2),
        name="conv",
    )(h, h, h, mod, g.reshape(1, d), chunk_major(w_in.astype(BF16)), chunk_major(b_in.reshape(1, 2 * d)),
      dw_pad, dw_b.reshape(1, d), ln_g.reshape(1, d), ln_b.reshape(1, d), w_out.astype(BF16), b_out.reshape(1, d))


def _gmlp_kernel(h_ref, mod_ref, g_ref, w_in_ref, b_in_ref, lng_ref, lnb_ref, ws_ref, bs_ref, w_out_ref,
                 o_ref, gate_ref, *, sub):
    t, d = h_ref.shape[1], h_ref.shape[2]
    gd = d // GMLP_GROUPS

    def project(r0):
        n = _norm_mod(h_ref[0, r0:r0 + sub, :], g_ref[...], mod_ref, 0).astype(BF16)
        return _dot(n, w_in_ref[...])

    def gate(r0, acc):
        z = jax.nn.gelu(acc + b_in_ref[...])
        u = z[:, :d]
        v = _layer_norm(z[:, d:], lng_ref[...], lnb_ref[...]).astype(BF16)
        for c in range(sub // GMLP_CHUNK):
            rs = slice(c * GMLP_CHUNK, (c + 1) * GMLP_CHUNK)
            for gi in range(GMLP_GROUPS):
                ls = slice(gi * gd, (gi + 1) * gd)
                sg = _dot(ws_ref[gi], v[rs, ls]) + jnp.tile(bs_ref[gi], (1, gd // LANES))
                gate_ref[r0 + c * GMLP_CHUNK:r0 + (c + 1) * GMLP_CHUNK, ls] = (u[rs, ls] * sg).astype(BF16)

    def project_out(r0):
        y = _dot(gate_ref[r0:r0 + sub, :], w_out_ref[...])
        o_ref[0, r0:r0 + sub, :] = h_ref[0, r0:r0 + sub, :] + mod_ref[0, 2:3, :] * y

    acc = project(0)
    for r0 in range(0, t, sub):
        nxt = project(r0 + sub) if r0 + sub < t else None
        gate(r0, acc)
        if r0 > 0:
            project_out(r0 - sub)
        acc = nxt
    project_out(t - sub)


def _chunk_gmlp(h, mod, g, w_in, b_in, ln_g, ln_b, w_s, b_s, w_out):
    b, s, d = h.shape
    t = 1024 if s % 1024 == 0 else (256 if s % 256 == 0 else GMLP_CHUNK)
    bs = jnp.broadcast_to(b_s[:, :, None], (GMLP_GROUPS, GMLP_CHUNK, LANES)).astype(F32)
    return pl.pallas_call(
        functools.partial(_gmlp_kernel, sub=min(t, 256)),
        out_shape=jax.ShapeDtypeStruct(h.shape, F32),
        grid=(b, s // t),
        in_specs=[pl.BlockSpec((1, t, d), lambda i, j: (i, j, 0)),
                  pl.BlockSpec((1, 6, d), lambda i, j: (i, 0, 0)),
                  _resident((1, d)), _resident((d, 2 * d)), _resident((1, 2 * d)),
                  _resident((1, d)), _resident((1, d)),
                  _resident((GMLP_GROUPS, GMLP_CHUNK, GMLP_CHUNK)), _resident((GMLP_GROUPS, GMLP_CHUNK, LANES)),
                  _resident((d, d))],
        out_specs=pl.BlockSpec((1, t, d), lambda i, j: (i, j, 0)),
        scratch_shapes=[pltpu.VMEM((t, d), BF16)],
        compiler_params=_params(2),
        name="gmlp",
    )(h, mod, g.reshape(1, d), w_in.astype(BF16), b_in.reshape(1, 2 * d), ln_g.reshape(1, d),
      ln_b.reshape(1, d), w_s.astype(BF16), bs, w_out.astype(BF16))


def kernel(x, c, ctx, c_ctx, ada_w, ada_b, norm1_g, norm2_g, mlp_w1, mlp_w2, fnet_w, fnet_b, attn_w_qkv, attn_q_g, attn_k_g, attn_sink, attn_w_o, conv_w_in, conv_b_in, conv_dw, conv_dw_b, conv_ln_g, conv_ln_b, conv_w_out, conv_b_out, gmlp_w_in, gmlp_b_in, gmlp_ln_g, gmlp_ln_b, gmlp_w_s, gmlp_b_s, gmlp_w_out):
    bsz, _, d = x.shape
    depth = ada_w.shape[0]
    n_rows = -(-(bsz + 1) // 8) * 8
    c_rows = jnp.concatenate([c, c_ctx[None, :], jnp.zeros((n_rows - bsz - 1, d), F32)], axis=0)
    mods = _adaln(c_rows, ada_w, ada_b).reshape(depth, n_rows, 6, d)

    w1_all = mlp_w1.astype(BF16)
    w2_all = mlp_w2.astype(BF16)
    h_lat, h_ctx = x, ctx
    for i in range(depth):
        last = i == depth - 1
        kind, li = i % N_MIXERS, i // N_MIXERS
        mod_lat = mods[i, :bsz]
        mod_ctx = jnp.broadcast_to(mods[i, bsz], (bsz, 6, d))
        streams = [(h_lat, mod_lat)] + ([] if last else [(h_ctx, mod_ctx)])
        mixed = []
        if kind == 1:
            w_perm = _permute_qkv_columns(attn_w_qkv[li], d)
            w_o = attn_w_o[li].astype(BF16)
            zero_b = jnp.zeros((d,), F32)
            q, k, v = _qkv(h_lat, mod_lat, norm1_g[i], w_perm, attn_q_g[li], attn_k_g[li], True)
            qc, kc, vc = _qkv(h_ctx, mod_ctx, norm1_g[i], w_perm, attn_q_g[li], attn_k_g[li], False)
            mixed.append((h_lat, (_attention(q, k, v, kc, vc, attn_sink[li], True), w_o, zero_b)))
            if not last:
                mixed.append((h_ctx, (_attention(qc, None, None, kc, vc, attn_sink[li], False), w_o, zero_b)))
        else:
            for h, mod in streams:
                if kind == 0:
                    mixed.append((h, (_fourier_mixer(h, mod, norm1_g[i]), fnet_w[li].astype(BF16), fnet_b[li])))
                elif kind == 2:
                    mixed.append((_conv_module(h, mod, norm1_g[i], conv_w_in[li], conv_b_in[li], conv_dw[li],
                                               conv_dw_b[li], conv_ln_g[li], conv_ln_b[li], conv_w_out[li],
                                               conv_b_out[li]), None))
                else:
                    mixed.append((_chunk_gmlp(h, mod, norm1_g[i], gmlp_w_in[li], gmlp_b_in[li], gmlp_ln_g[li],
                                              gmlp_ln_b[li], gmlp_w_s[li], gmlp_b_s[li], gmlp_w_out[li]), None))
        h_lat = _mlp(mixed[0][0], mod_lat, norm2_g[i], w1_all, w2_all, i, mixed[0][1])
        if not last:
            hc, pend = mixed[1]
            if pend is not None:
                pend = (pend[0].reshape(1, -1, pend[0].shape[-1]), pend[1], pend[2])
            h_ctx = _mlp(hc.reshape(1, -1, d), mod_ctx[:1], norm2_g[i], w1_all, w2_all, i, pend).reshape(hc.shape)
    return h_lat
```

```python
import functools
import math

import numpy as np
import jax
import jax.numpy as jnp
from jax import lax
from jax.experimental import pallas as pl
from jax.experimental.pallas import tpu as pltpu

F32 = jnp.float32
BF16 = jnp.bfloat16

NORM_EPS = 1e-6
GRID_W = 64
FNET_GROUPS = 4
HEAD_DIM = 64
KV_HEADS = 2
WINDOW = 128
ATTN_BLOCK = 128
ROPE_THETA = 10000.0
ROPE_PAIRS = HEAD_DIM // 4
CONV_WIDTH = 31
CONV_HALO = 16
GMLP_CHUNK = 128
GMLP_GROUPS = 4
N_MIXERS = 4
LANES = 128
SUBLANES = 8
VMEM_LIMIT = 56 * 1024 * 1024
NEG_BIG = -1e30
LOG2E = math.log2(math.e)


def _resident(shape):
    nd = len(shape)
    return pl.BlockSpec(shape, lambda *_: (0,) * nd, pipeline_mode=pl.Buffered(1))


def _params(n_grid):
    return pltpu.CompilerParams(dimension_semantics=("parallel",) * n_grid, vmem_limit_bytes=VMEM_LIMIT)


def _row_tile(s):
    return 512 if s % 512 == 0 else s


def _wide_row_tile(s):
    return 1024 if s % 1024 == 0 else _row_tile(s)


def _dot(a, b):
    return jnp.dot(a, b, preferred_element_type=F32)


def _dot_nt(a, b):
    return lax.dot_general(a, b, (((1,), (1,)), ((), ())), preferred_element_type=F32)


def _norm_mod(x, g, mod_ref, row):
    ms = jnp.mean(x * x, axis=-1, keepdims=True)
    gain = g * (1.0 + mod_ref[0, row + 1:row + 2, :])
    return x * lax.rsqrt(ms + NORM_EPS) * gain + mod_ref[0, row:row + 1, :]


def _pipeline_rows(t, sub, project, finish):
    acc = project(0)
    for r0 in range(0, t, sub):
        nxt = project(r0 + sub) if r0 + sub < t else None
        finish(r0, acc)
        acc = nxt


def _layer_norm(x, g, b):
    mu = jnp.mean(x, axis=-1, keepdims=True)
    xc = x - mu
    var = jnp.mean(xc * xc, axis=-1, keepdims=True)
    return xc * lax.rsqrt(var + NORM_EPS) * g + b


def _adaln_kernel(c_ref, w_ref, b_ref, o_ref):
    c = c_ref[...]
    s = (c * jax.nn.sigmoid(c)).astype(BF16)
    o_ref[0] = _dot(s, w_ref[0].astype(BF16)) + b_ref[0]


def _adaln(c_rows, ada_w, ada_b):
    depth, d, n = ada_w.shape
    r = c_rows.shape[0]
    tn = 1024
    return pl.pallas_call(
        _adaln_kernel,
        out_shape=jax.ShapeDtypeStruct((depth, r, n), F32),
        grid=(depth, n // tn),
        in_specs=[pl.BlockSpec((r, d), lambda l, j: (0, 0)),
                  pl.BlockSpec((1, d, tn), lambda l, j: (l, 0, j)),
                  pl.BlockSpec((1, 1, tn), lambda l, j: (l, 0, j))],
        out_specs=pl.BlockSpec((1, r, tn), lambda l, j: (l, 0, j)),
        compiler_params=_params(2),
        name="adaln",
    )(c_rows, ada_w, ada_b.reshape(depth, 1, n))


def _mlp_kernel(h_ref, mod_ref, g_ref, w1_ref, w2_ref, *refs, ff_chunk):
    x = h_ref[0]
    if len(refs) == 4:
        y_ref, wo_ref, bo_ref, o_ref = refs
        x = x + mod_ref[0, 2:3, :] * (_dot(y_ref[0], wo_ref[...]) + bo_ref[...])
    else:
        o_ref, = refs
    n = _norm_mod(x, g_ref[...], mod_ref, 3).astype(BF16)
    d_ff = w1_ref.shape[2]
    acc = jnp.zeros(x.shape, F32)
    for c in range(d_ff // ff_chunk):
        a = _dot(n, w1_ref[0, :, c * ff_chunk:(c + 1) * ff_chunk])
        a = jnp.square(jnp.maximum(a, 0.0)).astype(BF16)
        acc = acc + _dot(a, w2_ref[0, c * ff_chunk:(c + 1) * ff_chunk, :])
    o_ref[0] = x + mod_ref[0, 5:6, :] * acc


def _mlp(h, mod, g, w1, w2, layer, pending=None):
    b, s, d = h.shape
    t = _wide_row_tile(s)
    d_ff = w1.shape[2]
    in_specs = [pl.BlockSpec((1, t, d), lambda i, j: (i, j, 0)),
                pl.BlockSpec((1, 6, d), lambda i, j: (i, 0, 0)),
                _resident((1, d)),
                pl.BlockSpec((1, d, d_ff), lambda i, j: (layer, 0, 0), pipeline_mode=pl.Buffered(1)),
                pl.BlockSpec((1, d_ff, d), lambda i, j: (layer, 0, 0), pipeline_mode=pl.Buffered(1))]
    args = [h, mod, g.reshape(1, d), w1, w2]
    if pending is not None:
        y, wo, bo = pending
        k = y.shape[-1]
        in_specs += [pl.BlockSpec((1, t, k), lambda i, j: (i, j, 0)), _resident((k, d)), _resident((1, d))]
        args += [y, wo, bo.reshape(1, d)]
    return pl.pallas_call(
        functools.partial(_mlp_kernel, ff_chunk=1024),
        out_shape=jax.ShapeDtypeStruct(h.shape, F32),
        grid=(b, s // t),
        in_specs=in_specs,
        out_specs=pl.BlockSpec((1, t, d), lambda i, j: (i, j, 0)),
        compiler_params=_params(2),
        name="mlp",
    )(*args)


def _scaled_sum(ca, a, cb, b):
    mag = abs(ca)
    if ca > 0 and cb > 0:
        return mag * (a + b)
    if ca > 0:
        return mag * (a - b)
    if cb > 0:
        return mag * (b - a)
    return (-mag) * (a + b)


def _small_dft(z):
    r = len(z)
    if r == 1:
        return z
    even = _small_dft(z[0::2])
    odd = _small_dft(z[1::2])
    out = [None] * r
    for k in range(r // 2):
        (er, ei), (a, b) = even[k], odd[k]
        if k == 0:
            tr, ti = a, b
        elif 4 * k == r:
            out[k] = (er + b, ei - a)
            out[k + r // 2] = (er - b, ei + a)
            continue
        else:
            c, s = math.cos(2 * math.pi * k / r), math.sin(2 * math.pi * k / r)
            if abs(abs(c) - abs(s)) < 1e-12:
                tr, ti = _scaled_sum(c, a, s, b), _scaled_sum(c, b, -s, a)
            else:
                tr, ti = c * a + s * b, c * b - s * a
        out[k] = (er + tr, ei + ti)
        out[k + r // 2] = (er - tr, ei - ti)
    return out


def _fnet_chan_kernel(*refs, gc, radix):
    h_refs = refs[:radix]
    mod_ref, g_ref, cs_ref, tw_ref, yr_ref, yi_ref, n_ref = refs[radix:]
    d = h_refs[0].shape[2]
    for q in range(radix):
        n_ref[q] = _norm_mod(h_refs[q][0], g_ref[...], mod_ref, 0).astype(BF16)
    for gi in range(d // gc):
        ls = slice(gi * gc, (gi + 1) * gc)
        z = []
        for q in range(radix):
            r = _dot(n_ref[q, :, ls], cs_ref[...])
            z.append((r[:, :gc], r[:, gc:]))
        for p, (br, bi) in enumerate(_small_dft(z)):
            if p > 0:
                c = jnp.tile(tw_ref[p - 1, 0], (1, gc // LANES))
                s = jnp.tile(tw_ref[p - 1, 1], (1, gc // LANES))
                br, bi = c * br + s * bi, c * bi - s * br
            yr_ref[0, p, :, ls] = br.astype(BF16)
            yi_ref[0, p, :, ls] = bi.astype(BF16)


def _fnet_pos_kernel(yr_ref, yi_ref, csm_ref, o_ref, x_ref, *, radix, m):
    tc = yr_ref.shape[3]
    for p in range(radix):
        x = _dot(csm_ref[...], jnp.concatenate([yr_ref[0, p], yi_ref[0, p]], axis=0))
        if radix == 1:
            o_ref[0] = x.astype(BF16)
        else:
            for cb in range(tc // LANES):
                x_ref[cb, pl.ds(p, m, stride=radix), :] = x[:, cb * LANES:(cb + 1) * LANES]
    if radix > 1:
        for cb in range(tc // LANES):
            o_ref[0, :, cb * LANES:(cb + 1) * LANES] = x_ref[cb].astype(BF16)


def _fnet_tables(s, gc):
    radix = 8 if s % 1024 == 0 else 1
    m = s // radix
    ang_c = 2.0 * np.pi * np.outer(np.arange(gc), np.arange(gc)) / gc
    cs = np.concatenate([np.cos(ang_c), -np.sin(ang_c)], axis=1)
    ang_m = 2.0 * np.pi * np.outer(np.arange(m), np.arange(m)) / m
    csm = np.concatenate([np.cos(ang_m), np.sin(ang_m)], axis=1) / math.sqrt(s * gc)
    if radix > 1:
        ang_t = 2.0 * np.pi * np.outer(np.arange(1, radix), np.arange(m)) / s
        tw = np.stack([np.cos(ang_t), np.sin(ang_t)], axis=1)
        tw = np.broadcast_to(tw[..., None], (radix - 1, 2, m, LANES))
    else:
        tw = np.zeros((1, 2, m, LANES))
    return (radix, m, jnp.asarray(cs, BF16), jnp.asarray(csm, BF16), jnp.asarray(tw, F32))


def _fourier_mixer(h, mod, g):
    b, s, d = h.shape
    gc = d // FNET_GROUPS
    radix, m, cs, csm, tw = _fnet_tables(s, gc)
    tm = min(m, 256)
    steps = m // tm
    h_specs = [pl.BlockSpec((1, tm, d), lambda i, j, q=q: (i, q * steps + j, 0)) for q in range(radix)]
    y_spec = pl.BlockSpec((1, radix, tm, d), lambda i, j: (i, 0, j, 0))
    yr, yi = pl.pallas_call(
        functools.partial(_fnet_chan_kernel, gc=gc, radix=radix),
        out_shape=(jax.ShapeDtypeStruct((b, radix, m, d), BF16),) * 2,
        grid=(b, steps),
        in_specs=h_specs + [pl.BlockSpec((1, 6, d), lambda i, j: (i, 0, 0)),
                            _resident((1, d)), _resident((gc, 2 * gc)),
                            pl.BlockSpec((tw.shape[0], 2, tm, LANES), lambda i, j: (0, 0, j, 0))],
        out_specs=(y_spec, y_spec),
        scratch_shapes=[pltpu.VMEM((radix, tm, d), BF16)],
        compiler_params=_params(2),
        name="fnet_chan",
    )(*([h] * radix), mod, g.reshape(1, d), cs, tw)
    tc = 256 if radix > 1 else d
    return pl.pallas_call(
        functools.partial(_fnet_pos_kernel, radix=radix, m=m),
        out_shape=jax.ShapeDtypeStruct((b, s, d), BF16),
        grid=(b, d // tc),
        in_specs=[pl.BlockSpec((1, radix, m, tc), lambda i, j: (i, 0, 0, j)),
                  pl.BlockSpec((1, radix, m, tc), lambda i, j: (i, 0, 0, j)),
                  _resident((m, 2 * m))],
        out_specs=pl.BlockSpec((1, s, tc), lambda i, j: (i, 0, j)),
        scratch_shapes=[pltpu.VMEM((tc // LANES, s if radix > 1 else SUBLANES, LANES), F32)],
        compiler_params=_params(2),
        name="fnet_pos",
    )(yr, yi, csm)


def _qkv_kernel(h_ref, mod_ref, g_ref, w_ref, same_ref, hg_ref, cos_ref, sin_ref,
                q_ref, k_ref, v_ref, *, nq, nk, sub):
    t = h_ref.shape[1]

    def project(r0):
        n = _norm_mod(h_ref[0, r0:r0 + sub, :], g_ref[...], mod_ref, 0).astype(BF16)
        return _dot(n, w_ref[...])

    def epilogue(r0, acc):
        qk = acc[:, :nq + nk]
        sq = (qk * qk).astype(BF16)
        width = same_ref.shape[0]
        ssq = jnp.concatenate([_dot(sq[:, c:c + width], same_ref[...]) for c in range(0, nq + nk, width)], axis=1)
        qk = qk * lax.rsqrt(ssq * (1.0 / HEAD_DIM) + NORM_EPS) * hg_ref[...]
        cos = cos_ref[r0:r0 + sub, :]
        sin = sin_ref[r0:r0 + sub, :]
        for c in range((nq + nk) // LANES):
            x = qk[:, c * LANES:(c + 1) * LANES]
            y = (x * cos + pltpu.roll(x, LANES // 2, 1) * sin).astype(BF16)
            if c < nq // LANES:
                q_ref[0, r0:r0 + sub, c * LANES:(c + 1) * LANES] = y
            else:
                k_ref[0, r0:r0 + sub, c * LANES - nq:(c + 1) * LANES - nq] = y
        v_ref[0, r0:r0 + sub, :] = acc[:, nq + nk:].astype(BF16)

    _pipeline_rows(t, sub, project, epilogue)


def _to_block_lanes(x, slots):
    lead = x.shape[:-1]
    x = x.reshape(*lead, -1, slots, 2, 2, ROPE_PAIRS)
    x = jnp.moveaxis(x, -2, -4)
    if slots == 1:
        x = jnp.broadcast_to(x, (*x.shape[:-3], 2, 2, ROPE_PAIRS))
    return x.reshape(*lead, -1)


def _rope_tables(s, with_rope):
    if not with_rope:
        return jnp.ones((s, LANES), F32), jnp.zeros((s, LANES), F32)
    rows = s // GRID_W
    row = jnp.repeat(jnp.arange(rows, dtype=F32), GRID_W)
    col = jnp.tile(jnp.arange(GRID_W, dtype=F32), rows)
    inv = ROPE_THETA ** (-jnp.arange(ROPE_PAIRS, dtype=F32) / ROPE_PAIRS)
    ang_r = row[:, None] * inv[None, :]
    ang_c = col[:, None] * inv[None, :]
    cos = jnp.concatenate([jnp.cos(ang_r), jnp.cos(ang_c)], axis=1)
    sin = jnp.concatenate([jnp.sin(ang_r), jnp.sin(ang_c)], axis=1)
    reps = LANES // (4 * ROPE_PAIRS)
    return jnp.tile(cos, (1, 2 * reps)), jnp.concatenate([jnp.tile(-sin, (1, reps)), jnp.tile(sin, (1, reps))], axis=1)


def _qkv(h, mod, g, w_perm, q_g, k_g, with_rope):
    b, s, d = h.shape
    nq = d
    nk = KV_HEADS * LANES
    t = _wide_row_tile(s)
    width = 2 * LANES
    lane = np.arange(width)
    head = 2 * (lane // LANES) + (lane % HEAD_DIM) // (HEAD_DIM // 2)
    same = (head[:, None] == head[None, :]).astype(np.float32)
    assert (nq + nk) % width == 0
    q_gain = _to_block_lanes(jnp.tile(q_g * (HEAD_DIM ** -0.5 * LOG2E), 2), 2)
    hg = jnp.concatenate([jnp.tile(q_gain, nq // LANES), jnp.tile(_to_block_lanes(k_g, 1), nk // LANES)])
    hg = hg.reshape(1, nq + nk)
    cos, sin = _rope_tables(s, with_rope)
    return pl.pallas_call(
        functools.partial(_qkv_kernel, nq=nq, nk=nk, sub=min(t, 256)),
        out_shape=(jax.ShapeDtypeStruct((b, s, nq), BF16),
                   jax.ShapeDtypeStruct((b, s, nk), BF16),
                   jax.ShapeDtypeStruct((b, s, nk), BF16)),
        grid=(b, s // t),
        in_specs=[pl.BlockSpec((1, t, d), lambda i, j: (i, j, 0)),
                  pl.BlockSpec((1, 6, d), lambda i, j: (i, 0, 0)),
                  _resident((1, d)), _resident(w_perm.shape),
                  _resident((width, width)), _resident((1, nq + nk)),
                  pl.BlockSpec((t, LANES), lambda i, j: (j, 0)),
                  pl.BlockSpec((t, LANES), lambda i, j: (j, 0))],
        out_specs=(pl.BlockSpec((1, t, nq), lambda i, j: (i, j, 0)),
                   pl.BlockSpec((1, t, nk), lambda i, j: (i, j, 0)),
                   pl.BlockSpec((1, t, nk), lambda i, j: (i, j, 0))),
        compiler_params=_params(2),
        name="qkv",
    )(h, mod, g.reshape(1, d), w_perm, jnp.asarray(same, BF16), hg, cos, sin)


def _attn_kernel(sink_ref, q_ref, *refs, band, s_len, n_ctx):
    if band:
        k_ref, v_ref, kc_ref, vc_ref, o_ref = refs
    else:
        kc_ref, vc_ref, o_ref = refs
    nq = ATTN_BLOCK
    n_blocks = q_ref.shape[1] // nq
    n_band = 3 * ATTN_BLOCK if band else 0
    n_keys = n_band + n_ctx
    group = q_ref.shape[2] // (KV_HEADS * HEAD_DIM)
    pairs = group // 2
    lane = lax.broadcasted_iota(jnp.int32, (1, LANES), 1)
    low = lane < HEAD_DIM
    first = (lane % HEAD_DIM) < HEAD_DIM // 2

    bias_band, k_bd, v_bd = [], [], []
    for qb in range(n_blocks):
        j = pl.program_id(1) * n_blocks + qb
        if band:
            start = pl.multiple_of(jnp.clip((j - 1) * ATTN_BLOCK, 0, s_len - n_band), ATTN_BLOCK)
            qpos = j * ATTN_BLOCK + lax.broadcasted_iota(jnp.int32, (nq, n_band), 0)
            kpos = start + lax.broadcasted_iota(jnp.int32, (nq, n_band), 1)
            bias_band.append(jnp.where(jnp.abs(qpos - kpos) <= WINDOW, 0.0, NEG_BIG).astype(F32))
        for kv in range(KV_HEADS):
            lanes_kv = slice(kv * LANES, (kv + 1) * LANES)
            k_all = kc_ref[0, :, lanes_kv]
            v_all = vc_ref[0, :, lanes_kv]
            if band:
                k_all = jnp.concatenate([k_ref[0, pl.ds(start, n_band), lanes_kv], k_all], axis=0)
                v_all = jnp.concatenate([v_ref[0, pl.ds(start, n_band), lanes_kv], v_all], axis=0)
            zero = jnp.zeros_like(k_all)
            k_bd.append(jnp.concatenate([jnp.where(first, k_all, zero), jnp.where(first, zero, k_all)], axis=0))
            v_bd.append(jnp.concatenate([jnp.where(low, v_all, zero), jnp.where(low, zero, v_all)], axis=0))

    def logits(qb, kv, m):
        blk = slice((kv * pairs + m) * LANES, (kv * pairs + m + 1) * LANES)
        return _dot_nt(q_ref[0, qb * nq:(qb + 1) * nq, blk], k_bd[qb * KV_HEADS + kv])

    def softmax(qb, kv, m, sc):
        probs, denoms = [], []
        for half in range(2):
            sh = sc[:, half * n_keys:(half + 1) * n_keys]
            if band:
                sh = jnp.concatenate([sh[:, :n_band] + bias_band[qb], sh[:, n_band:]], axis=1)
            sink = sink_ref[kv * group + 2 * m + half] * LOG2E
            mx = jnp.maximum(jnp.max(sh, axis=1, keepdims=True), sink)
            e = jnp.exp2(sh - mx)
            denoms.append(jnp.sum(e, axis=1, keepdims=True) + jnp.exp2(sink - mx))
            probs.append(e.astype(BF16))
        return jnp.concatenate(probs, axis=1), jnp.where(low, denoms[0], denoms[1])

    def weighted_values(qb, kv, m, p, denom):
        blk = slice((kv * pairs + m) * LANES, (kv * pairs + m + 1) * LANES)
        o_ref[0, qb * nq:(qb + 1) * nq, blk] = (_dot(p, v_bd[qb * KV_HEADS + kv]) / denom).astype(BF16)

    items = [(qb, kv, m) for qb in range(n_blocks) for kv in range(KV_HEADS) for m in range(pairs)]
    sc = logits(*items[0])
    pending = None
    for idx, item in enumerate(items):
        nxt = logits(*items[idx + 1]) if idx + 1 < len(items) else None
        p, denom = softmax(*item, sc)
        if pending is not None:
            weighted_values(*pending)
        pending = (*item, p, denom)
        sc = nxt
    weighted_values(*pending)


def _attention(q, k, v, kc, vc, sink, band):
    b, s, d = q.shape
    n_ctx = kc.shape[1]
    nkv = kc.shape[2]
    tq = next(n * ATTN_BLOCK for n in (8, 4, 2, 1) if s % (n * ATTN_BLOCK) == 0)
    in_specs = [pl.BlockSpec(memory_space=pltpu.SMEM),
                pl.BlockSpec((1, tq, d), lambda i, j: (i, j, 0))]
    args = [sink.astype(F32), q]
    if band:
        in_specs += [pl.BlockSpec((1, s, nkv), lambda i, j: (i, 0, 0))] * 2
        args += [k, v]
    in_specs += [pl.BlockSpec((1, n_ctx, nkv), lambda i, j: (i, 0, 0))] * 2
    args += [kc, vc]
    return pl.pallas_call(
        functools.partial(_attn_kernel, band=band, s_len=s, n_ctx=n_ctx),
        out_shape=jax.ShapeDtypeStruct((b, s, d), BF16),
        grid=(b, s // tq),
        in_specs=in_specs,
        out_specs=pl.BlockSpec((1, tq, d), lambda i, j: (i, j, 0)),
        compiler_params=_params(2),
        name="attn_band" if band else "attn_ctx",
    )(*args)


def _permute_qkv_columns(w_qkv, d):
    w = w_qkv.astype(BF16)
    nkv = KV_HEADS * HEAD_DIM
    wv = w[:, d + nkv:].reshape(-1, KV_HEADS, 1, HEAD_DIM)
    wv = jnp.broadcast_to(wv, (w.shape[0], KV_HEADS, LANES // HEAD_DIM, HEAD_DIM)).reshape(w.shape[0], -1)
    return jnp.concatenate([_to_block_lanes(w[:, :d], 2), _to_block_lanes(w[:, d:d + nkv], 1), wv], axis=1)


def _conv_kernel(h_ref, hp_ref, hn_ref, mod_ref, g_ref, w_in_ref, b_in_ref, dw_ref, dwb_ref, lng_ref,
                       lnb_ref, w_ref, b_ref, o_ref, n_ref, ext_ref, y_ref, *, rows):
    j = pl.program_id(1)
    t, d = h_ref.shape[1], h_ref.shape[2]
    n_chunks = d // LANES
    halo = CONV_HALO
    n_ref[0:halo, :] = _norm_mod(hp_ref[0], g_ref[...], mod_ref, 0).astype(BF16)
    n_ref[halo:halo + t, :] = _norm_mod(h_ref[0], g_ref[...], mod_ref, 0).astype(BF16)
    n_ref[halo + t:, :] = _norm_mod(hn_ref[0], g_ref[...], mod_ref, 0).astype(BF16)
    shift = halo - CONV_WIDTH // 2

    def glu(c):
        a = _dot(n_ref[...], w_in_ref[:, 2 * c * LANES:2 * (c + 1) * LANES]) + b_in_ref[:, 2 * c * LANES:2 * (c + 1) * LANES]
        u = a[:, :LANES] * jax.nn.sigmoid(a[:, LANES:])
        ext_ref[c, 0:halo, :] = jnp.where(j > 0, u[0:halo], 0.0)
        ext_ref[c, halo:halo + t, :] = u[halo:halo + t]
        ext_ref[c, halo + t:, :] = jnp.where(j < pl.num_programs(1) - 1, u[halo + t:], 0.0)

    def conv(c):
        for r0 in range(0, t, rows):
            acc = jnp.zeros((rows, LANES), F32)
            for tap in range(CONV_WIDTH):
                acc = acc + dw_ref[c, tap:tap + 1, :] * ext_ref[c, r0 + tap + shift:r0 + tap + shift + rows, :]
            y_ref[c, r0:r0 + rows, :] = acc

    glu(0)
    for c in range(n_chunks):
        if c + 1 < n_chunks:
            glu(c + 1)
        conv(c)
    y = jnp.concatenate([y_ref[c] for c in range(n_chunks)], axis=1)
    y = _layer_norm(y + dwb_ref[...], lng_ref[...], lnb_ref[...])
    y = (y * jax.nn.sigmoid(y)).astype(BF16)
    acc = _dot(y, w_ref[...]) + b_ref[...]
    o_ref[0] = h_ref[0] + mod_ref[0, 2:3, :] * acc


def _conv_module(h, mod, g, w_in, b_in, dw, dw_b, ln_g, ln_b, w_out, b_out):
    b, s, d = h.shape
    t = _wide_row_tile(s)
    hb = t // CONV_HALO
    last = s // CONV_HALO - 1
    n_chunks = d // LANES
    taps_pad = -(-CONV_WIDTH // SUBLANES) * SUBLANES
    dw_pad = jnp.concatenate([dw, jnp.zeros((taps_pad - CONV_WIDTH, d), F32)], axis=0)
    dw_pad = dw_pad.reshape(taps_pad, n_chunks, LANES).transpose(1, 0, 2)

    def chunk_major(x):
        lead = x.shape[:-1]
        return x.reshape(*lead, 2, n_chunks, LANES).swapaxes(-2, -3).reshape(*lead, 2 * d)

    return pl.pallas_call(
        functools.partial(_conv_kernel, rows=64),
        out_shape=jax.ShapeDtypeStruct(h.shape, F32),
        grid=(b, s // t),
        in_specs=[pl.BlockSpec((1, t, d), lambda i, j: (i, j, 0)),
                  pl.BlockSpec((1, CONV_HALO, d), lambda i, j: (i, jnp.maximum(j * hb - 1, 0), 0)),
                  pl.BlockSpec((1, CONV_HALO, d), lambda i, j: (i, jnp.minimum((j + 1) * hb, last), 0)),
                  pl.BlockSpec((1, 6, d), lambda i, j: (i, 0, 0)),
                  _resident((1, d)), _resident((d, 2 * d)), _resident((1, 2 * d)),
                  _resident((n_chunks, taps_pad, LANES)), _resident((1, d)), _resident((1, d)), _resident((1, d)),
                  _resident((d, d)), _resident((1, d))],
        out_specs=pl.BlockSpec((1, t, d), lambda i, j: (i, j, 0)),
        scratch_shapes=[pltpu.VMEM((t + 2 * CONV_HALO, d), BF16),
                        pltpu.VMEM((n_chunks, t + 2 * CONV_HALO, LANES), F32),
                        pltpu.VMEM((n_chunks, t, LANES), F32)],
        compiler_params=_params(2),
        name="conv",
    )(h, h, h, mod, g.reshape(1, d), chunk_major(w_in.astype(BF16)), chunk_major(b_in.reshape(1, 2 * d)),
      dw_pad, dw_b.reshape(1, d), ln_g.reshape(1, d), ln_b.reshape(1, d), w_out.astype(BF16), b_out.reshape(1, d))


def _gmlp_kernel(h_ref, mod_ref, g_ref, w_in_ref, b_in_ref, lng_ref, lnb_ref, ws_ref, bs_ref, w_out_ref,
                 o_ref, gate_ref, *, sub):
    t, d = h_ref.shape[1], h_ref.shape[2]
    gd = d // GMLP_GROUPS

    def project(r0):
        n = _norm_mod(h_ref[0, r0:r0 + sub, :], g_ref[...], mod_ref, 0).astype(BF16)
        return _dot(n, w_in_ref[...])

    def gate(r0, acc):
        z = jax.nn.gelu(acc + b_in_ref[...])
        u = z[:, :d]
        v = _layer_norm(z[:, d:], lng_ref[...], lnb_ref[...]).astype(BF16)
        for c in range(sub // GMLP_CHUNK):
            rs = slice(c * GMLP_CHUNK, (c + 1) * GMLP_CHUNK)
            for gi in range(GMLP_GROUPS):
                ls = slice(gi * gd, (gi + 1) * gd)
                sg = _dot(ws_ref[gi], v[rs, ls]) + jnp.tile(bs_ref[gi], (1, gd // LANES))
                gate_ref[r0 + c * GMLP_CHUNK:r0 + (c + 1) * GMLP_CHUNK, ls] = (u[rs, ls] * sg).astype(BF16)

    def project_out(r0):
        y = _dot(gate_ref[r0:r0 + sub, :], w_out_ref[...])
        o_ref[0, r0:r0 + sub, :] = h_ref[0, r0:r0 + sub, :] + mod_ref[0, 2:3, :] * y

    acc = project(0)
    for r0 in range(0, t, sub):
        nxt = project(r0 + sub) if r0 + sub < t else None
        gate(r0, acc)
        if r0 > 0:
            project_out(r0 - sub)
        acc = nxt
    project_out(t - sub)


def _chunk_gmlp(h, mod, g, w_in, b_in, ln_g, ln_b, w_s, b_s, w_out):
    b, s, d = h.shape
    t = 1024 if s % 1024 == 0 else (256 if s % 256 == 0 else GMLP_CHUNK)
    bs = jnp.broadcast_to(b_s[:, :, None], (GMLP_GROUPS, GMLP_CHUNK, LANES)).astype(F32)
    return pl.pallas_call(
        functools.partial(_gmlp_kernel, sub=min(t, 256)),
        out_shape=jax.ShapeDtypeStruct(h.shape, F32),
        grid=(b, s // t),
        in_specs=[pl.BlockSpec((1, t, d), lambda i, j: (i, j, 0)),
                  pl.BlockSpec((1, 6, d), lambda i, j: (i, 0, 0)),
                  _resident((1, d)), _resident((d, 2 * d)), _resident((1, 2 * d)),
                  _resident((1, d)), _resident((1, d)),
                  _resident((GMLP_GROUPS, GMLP_CHUNK, GMLP_CHUNK)), _resident((GMLP_GROUPS, GMLP_CHUNK, LANES)),
                  _resident((d, d))],
        out_specs=pl.BlockSpec((1, t, d), lambda i, j: (i, j, 0)),
        scratch_shapes=[pltpu.VMEM((t, d), BF16)],
        compiler_params=_params(2),
        name="gmlp",
    )(h, mod, g.reshape(1, d), w_in.astype(BF16), b_in.reshape(1, 2 * d), ln_g.reshape(1, d),
      ln_b.reshape(1, d), w_s.astype(BF16), bs, w_out.astype(BF16))


def kernel(x, c, ctx, c_ctx, ada_w, ada_b, norm1_g, norm2_g, mlp_w1, mlp_w2, fnet_w, fnet_b, attn_w_qkv, attn_q_g, attn_k_g, attn_sink, attn_w_o, conv_w_in, conv_b_in, conv_dw, conv_dw_b, conv_ln_g, conv_ln_b, conv_w_out, conv_b_out, gmlp_w_in, gmlp_b_in, gmlp_ln_g, gmlp_ln_b, gmlp_w_s, gmlp_b_s, gmlp_w_out):
    bsz, _, d = x.shape
    depth = ada_w.shape[0]
    n_rows = -(-(bsz + 1) // 8) * 8
    c_rows = jnp.concatenate([c, c_ctx[None, :], jnp.zeros((n_rows - bsz - 1, d), F32)], axis=0)
    mods = _adaln(c_rows, ada_w, ada_b).reshape(depth, n_rows, 6, d)

    w1_all = mlp_w1.astype(BF16)
    w2_all = mlp_w2.astype(BF16)
    h_lat, h_ctx = x, ctx
    for i in range(depth):
        last = i == depth - 1
        kind, li = i % N_MIXERS, i // N_MIXERS
        mod_lat = mods[i, :bsz]
        mod_ctx = jnp.broadcast_to(mods[i, bsz], (bsz, 6, d))
        streams = [(h_lat, mod_lat)] + ([] if last else [(h_ctx, mod_ctx)])
        mixed = []
        if kind == 1:
            w_perm = _permute_qkv_columns(attn_w_qkv[li], d)
            w_o = attn_w_o[li].astype(BF16)
            zero_b = jnp.zeros((d,), F32)
            q, k, v = _qkv(h_lat, mod_lat, norm1_g[i], w_perm, attn_q_g[li], attn_k_g[li], True)
            qc, kc, vc = _qkv(h_ctx, mod_ctx, norm1_g[i], w_perm, attn_q_g[li], attn_k_g[li], False)
            mixed.append((h_lat, (_attention(q, k, v, kc, vc, attn_sink[li], True), w_o, zero_b)))
            if not last:
                mixed.append((h_ctx, (_attention(qc, None, None, kc, vc, attn_sink[li], False), w_o, zero_b)))
        else:
            for h, mod in streams:
                if kind == 0:
                    mixed.append((h, (_fourier_mixer(h, mod, norm1_g[i]), fnet_w[li].astype(BF16), fnet_b[li])))
                elif kind == 2:
                    mixed.append((_conv_module(h, mod, norm1_g[i], conv_w_in[li], conv_b_in[li], conv_dw[li],
                                               conv_dw_b[li], conv_ln_g[li], conv_ln_b[li], conv_w_out[li],
                                               conv_b_out[li]), None))
                else:
                    mixed.append((_chunk_gmlp(h, mod, norm1_g[i], gmlp_w_in[li], gmlp_b_in[li], gmlp_ln_g[li],
                                              gmlp_ln_b[li], gmlp_w_s[li], gmlp_b_s[li], gmlp_w_out[li]), None))
        h_lat = _mlp(mixed[0][0], mod_lat, norm2_g[i], w1_all, w2_all, i, mixed[0][1])
        if not last:
            hc, pend = mixed[1]
            if pend is not None:
                pend = (pend[0].reshape(1, -1, pend[0].shape[-1]), pend[1], pend[2])
            h_ctx = _mlp(hc.reshape(1, -1, d), mod_ctx[:1], norm2_g[i], w1_all, w2_all, i, pend).reshape(hc.shape)
    return h_lat
```

```python
import functools
import math

import numpy as np
import jax
import jax.numpy as jnp
from jax import lax
from jax.experimental import pallas as pl
from jax.experimental.pallas import tpu as pltpu

F32 = jnp.float32
BF16 = jnp.bfloat16

NORM_EPS = 1e-6
GRID_W = 64
FNET_GROUPS = 4
HEAD_DIM = 64
KV_HEADS = 2
WINDOW = 128
ATTN_BLOCK = 128
ROPE_THETA = 10000.0
ROPE_PAIRS = HEAD_DIM // 4
CONV_WIDTH = 31
CONV_HALO = 16
GMLP_CHUNK = 128
GMLP_GROUPS = 4
N_MIXERS = 4
LANES = 128
SUBLANES = 8
VMEM_LIMIT = 56 * 1024 * 1024
NEG_BIG = -1e30
LOG2E = math.log2(math.e)


def _resident(shape):
    nd = len(shape)
    return pl.BlockSpec(shape, lambda *_: (0,) * nd, pipeline_mode=pl.Buffered(1))


def _params(n_grid):
    return pltpu.CompilerParams(dimension_semantics=("parallel",) * n_grid, vmem_limit_bytes=VMEM_LIMIT)


def _row_tile(s):
    return 512 if s % 512 == 0 else s


def _wide_row_tile(s):
    return 1024 if s % 1024 == 0 else _row_tile(s)


def _dot(a, b):
    return jnp.dot(a, b, preferred_element_type=F32)


def _dot_nt(a, b):
    return lax.dot_general(a, b, (((1,), (1,)), ((), ())), preferred_element_type=F32)


def _norm_mod(x, g, mod_ref, row):
    ms = jnp.mean(x * x, axis=-1, keepdims=True)
    gain = g * (1.0 + mod_ref[0, row + 1:row + 2, :])
    return x * lax.rsqrt(ms + NORM_EPS) * gain + mod_ref[0, row:row + 1, :]


def _pipeline_rows(t, sub, project, finish):
    acc = project(0)
    for r0 in range(0, t, sub):
        nxt = project(r0 + sub) if r0 + sub < t else None
        finish(r0, acc)
        acc = nxt


def _layer_norm(x, g, b):
    mu = jnp.mean(x, axis=-1, keepdims=True)
    xc = x - mu
    var = jnp.mean(xc * xc, axis=-1, keepdims=True)
    return xc * lax.rsqrt(var + NORM_EPS) * g + b


def _adaln_kernel(c_ref, w_ref, b_ref, o_ref):
    c = c_ref[...]
    s = (c * jax.nn.sigmoid(c)).astype(BF16)
    o_ref[0] = _dot(s, w_ref[0].astype(BF16)) + b_ref[0]


def _adaln(c_rows, ada_w, ada_b):
    depth, d, n = ada_w.shape
    r = c_rows.shape[0]
    tn = 1024
    return pl.pallas_call(
        _adaln_kernel,
        out_shape=jax.ShapeDtypeStruct((depth, r, n), F32),
        grid=(depth, n // tn),
        in_specs=[pl.BlockSpec((r, d), lambda l, j: (0, 0)),
                  pl.BlockSpec((1, d, tn), lambda l, j: (l, 0, j)),
                  pl.BlockSpec((1, 1, tn), lambda l, j: (l, 0, j))],
        out_specs=pl.BlockSpec((1, r, tn), lambda l, j: (l, 0, j)),
        compiler_params=_params(2),
        name="adaln",
    )(c_rows, ada_w, ada_b.reshape(depth, 1, n))


def _mlp_kernel(h_ref, mod_ref, g_ref, w1_ref, w2_ref, *refs, ff_chunk):
    x = h_ref[0]
    if len(refs) == 4:
        y_ref, wo_ref, bo_ref, o_ref = refs
        x = x + mod_ref[0, 2:3, :] * (_dot(y_ref[0], wo_ref[...]) + bo_ref[...])
    else:
        o_ref, = refs
    n = _norm_mod(x, g_ref[...], mod_ref, 3).astype(BF16)
    d_ff = w1_ref.shape[2]
    acc = jnp.zeros(x.shape, F32)
    for c in range(d_ff // ff_chunk):
        a = _dot(n, w1_ref[0, :, c * ff_chunk:(c + 1) * ff_chunk])
        a = jnp.square(jnp.maximum(a, 0.0)).astype(BF16)
        acc = acc + _dot(a, w2_ref[0, c * ff_chunk:(c + 1) * ff_chunk, :])
    o_ref[0] = x + mod_ref[0, 5:6, :] * acc


def _mlp(h, mod, g, w1, w2, layer, pending=None):
    b, s, d = h.shape
    t = _wide_row_tile(s)
    d_ff = w1.shape[2]
    in_specs = [pl.BlockSpec((1, t, d), lambda i, j: (i, j, 0)),
                pl.BlockSpec((1, 6, d), lambda i, j: (i, 0, 0)),
                _resident((1, d)),
                pl.BlockSpec((1, d, d_ff), lambda i, j: (layer, 0, 0), pipeline_mode=pl.Buffered(1)),
                pl.BlockSpec((1, d_ff, d), lambda i, j: (layer, 0, 0), pipeline_mode=pl.Buffered(1))]
    args = [h, mod, g.reshape(1, d), w1, w2]
    if pending is not None:
        y, wo, bo = pending
        k = y.shape[-1]
        in_specs += [pl.BlockSpec((1, t, k), lambda i, j: (i, j, 0)), _resident((k, d)), _resident((1, d))]
        args += [y, wo, bo.reshape(1, d)]
    return pl.pallas_call(
        functools.partial(_mlp_kernel, ff_chunk=1024),
        out_shape=jax.ShapeDtypeStruct(h.shape, F32),
        grid=(b, s // t),
        in_specs=in_specs,
        out_specs=pl.BlockSpec((1, t, d), lambda i, j: (i, j, 0)),
        compiler_params=_params(2),
        name="mlp",
    )(*args)


def _scaled_sum(ca, a, cb, b):
    mag = abs(ca)
    if ca > 0 and cb > 0:
        return mag * (a + b)
    if ca > 0:
        return mag * (a - b)
    if cb > 0:
        return mag * (b - a)
    return (-mag) * (a + b)


def _small_dft(z):
    r = len(z)
    if r == 1:
        return z
    even = _small_dft(z[0::2])
    odd = _small_dft(z[1::2])
    out = [None] * r
    for k in range(r // 2):
        (er, ei), (a, b) = even[k], odd[k]
        if k == 0:
            tr, ti = a, b
        elif 4 * k == r:
            out[k] = (er + b, ei - a)
            out[k + r // 2] = (er - b, ei + a)
            continue
        else:
            c, s = math.cos(2 * math.pi * k / r), math.sin(2 * math.pi * k / r)
            if abs(abs(c) - abs(s)) < 1e-12:
                tr, ti = _scaled_sum(c, a, s, b), _scaled_sum(c, b, -s, a)
            else:
                tr, ti = c * a + s * b, c * b - s * a
        out[k] = (er + tr, ei + ti)
        out[k + r // 2] = (er - tr, ei - ti)
    return out


def _fnet_chan_kernel(*refs, gc, radix):
    h_refs = refs[:radix]
    mod_ref, g_ref, cs_ref, tw_ref, yr_ref, yi_ref, n_ref = refs[radix:]
    d = h_refs[0].shape[2]
    for q in range(radix):
        n_ref[q] = _norm_mod(h_refs[q][0], g_ref[...], mod_ref, 0).astype(BF16)
    for gi in range(d // gc):
        ls = slice(gi * gc, (gi + 1) * gc)
        z = []
        for q in range(radix):
            r = _dot(n_ref[q, :, ls], cs_ref[...])
            z.append((r[:, :gc], r[:, gc:]))
        for p, (br, bi) in enumerate(_small_dft(z)):
            if p > 0:
                c = jnp.tile(tw_ref[p - 1, 0], (1, gc // LANES))
                s = jnp.tile(tw_ref[p - 1, 1], (1, gc // LANES))
                br, bi = c * br + s * bi, c * bi - s * br
            yr_ref[0, p, :, ls] = br.astype(BF16)
            yi_ref[0, p, :, ls] = bi.astype(BF16)


def _fnet_pos_kernel(yr_ref, yi_ref, csm_ref, o_ref, x_ref, *, radix, m):
    tc = yr_ref.shape[3]
    for p in range(radix):
        x = _dot(csm_ref[...], jnp.concatenate([yr_ref[0, p], yi_ref[0, p]], axis=0))
        if radix == 1:
            o_ref[0] = x.astype(BF16)
        else:
            for cb in range(tc // LANES):
                x_ref[cb, pl.ds(p, m, stride=radix), :] = x[:, cb * LANES:(cb + 1) * LANES]
    if radix > 1:
        for cb in range(tc // LANES):
            o_ref[0, :, cb * LANES:(cb + 1) * LANES] = x_ref[cb].astype(BF16)


def _fnet_tables(s, gc):
    radix = 8 if s % 1024 == 0 else 1
    m = s // radix
    ang_c = 2.0 * np.pi * np.outer(np.arange(gc), np.arange(gc)) / gc
    cs = np.concatenate([np.cos(ang_c), -np.sin(ang_c)], axis=1)
    ang_m = 2.0 * np.pi * np.outer(np.arange(m), np.arange(m)) / m
    csm = np.concatenate([np.cos(ang_m), np.sin(ang_m)], axis=1) / math.sqrt(s * gc)
    if radix > 1:
        ang_t = 2.0 * np.pi * np.outer(np.arange(1, radix), np.arange(m)) / s
        tw = np.stack([np.cos(ang_t), np.sin(ang_t)], axis=1)
        tw = np.broadcast_to(tw[..., None], (radix - 1, 2, m, LANES))
    else:
        tw = np.zeros((1, 2, m, LANES))
    return (radix, m, jnp.asarray(cs, BF16), jnp.asarray(csm, BF16), jnp.asarray(tw, F32))


def _fourier_mixer(h, mod, g):
    b, s, d = h.shape
    gc = d // FNET_GROUPS
    radix, m, cs, csm, tw = _fnet_tables(s, gc)
    tm = min(m, 256)
    steps = m // tm
    h_specs = [pl.BlockSpec((1, tm, d), lambda i, j, q=q: (i, q * steps + j, 0)) for q in range(radix)]
    y_spec = pl.BlockSpec((1, radix, tm, d), lambda i, j: (i, 0, j, 0))
    yr, yi = pl.pallas_call(
        functools.partial(_fnet_chan_kernel, gc=gc, radix=radix),
        out_shape=(jax.ShapeDtypeStruct((b, radix, m, d), BF16),) * 2,
        grid=(b, steps),
        in_specs=h_specs + [pl.BlockSpec((1, 6, d), lambda i, j: (i, 0, 0)),
                            _resident((1, d)), _resident((gc, 2 * gc)),
                            pl.BlockSpec((tw.shape[0], 2, tm, LANES), lambda i, j: (0, 0, j, 0))],
        out_specs=(y_spec, y_spec),
        scratch_shapes=[pltpu.VMEM((radix, tm, d), BF16)],
        compiler_params=_params(2),
        name="fnet_chan",
    )(*([h] * radix), mod, g.reshape(1, d), cs, tw)
    tc = 256 if radix > 1 else d
    return pl.pallas_call(
        functools.partial(_fnet_pos_kernel, radix=radix, m=m),
        out_shape=jax.ShapeDtypeStruct((b, s, d), BF16),
        grid=(b, d // tc),
        in_specs=[pl.BlockSpec((1, radix, m, tc), lambda i, j: (i, 0, 0, j)),
                  pl.BlockSpec((1, radix, m, tc), lambda i, j: (i, 0, 0, j)),
                  _resident((m, 2 * m))],
        out_specs=pl.BlockSpec((1, s, tc), lambda i, j: (i, 0, j)),
        scratch_shapes=[pltpu.VMEM((tc // LANES, s if radix > 1 else SUBLANES, LANES), F32)],
        compiler_params=_params(2),
        name="fnet_pos",
    )(yr, yi, csm)


def _qkv_kernel(h_ref, mod_ref, g_ref, w_ref, same_ref, hg_ref, cos_ref, sin_ref,
                q_ref, k_ref, v_ref, *, nq, nk, sub):
    t = h_ref.shape[1]

    def project(r0):
        n = _norm_mod(h_ref[0, r0:r0 + sub, :], g_ref[...], mod_ref, 0).astype(BF16)
        return _dot(n, w_ref[...])

    def epilogue(r0, acc):
        qk = acc[:, :nq + nk]
        sq = (qk * qk).astype(BF16)
        width = same_ref.shape[0]
        ssq = jnp.concatenate([_dot(sq[:, c:c + width], same_ref[...]) for c in range(0, nq + nk, width)], axis=1)
        qk = qk * lax.rsqrt(ssq * (1.0 / HEAD_DIM) + NORM_EPS) * hg_ref[...]
        cos = cos_ref[r0:r0 + sub, :]
        sin = sin_ref[r0:r0 + sub, :]
        for c in range((nq + nk) // LANES):
            x = qk[:, c * LANES:(c + 1) * LANES]
            y = (x * cos + pltpu.roll(x, LANES // 2, 1) * sin).astype(BF16)
            if c < nq // LANES:
                q_ref[0, r0:r0 + sub, c * LANES:(c + 1) * LANES] = y
            else:
                k_ref[0, r0:r0 + sub, c * LANES - nq:(c + 1) * LANES - nq] = y
        v_ref[0, r0:r0 + sub, :] = acc[:, nq + nk:].astype(BF16)

    _pipeline_rows(t, sub, project, epilogue)


def _to_block_lanes(x, slots):
    lead = x.shape[:-1]
    x = x.reshape(*lead, -1, slots, 2, 2, ROPE_PAIRS)
    x = jnp.moveaxis(x, -2, -4)
    if slots == 1:
        x = jnp.broadcast_to(x, (*x.shape[:-3], 2, 2, ROPE_PAIRS))
    return x.reshape(*lead, -1)


def _rope_tables(s, with_rope):
    if not with_rope:
        return jnp.ones((s, LANES), F32), jnp.zeros((s, LANES), F32)
    rows = s // GRID_W
    row = jnp.repeat(jnp.arange(rows, dtype=F32), GRID_W)
    col = jnp.tile(jnp.arange(GRID_W, dtype=F32), rows)
    inv = ROPE_THETA ** (-jnp.arange(ROPE_PAIRS, dtype=F32) / ROPE_PAIRS)
    ang_r = row[:, None] * inv[None, :]
    ang_c = col[:, None] * inv[None, :]
    cos = jnp.concatenate([jnp.cos(ang_r), jnp.cos(ang_c)], axis=1)
    sin = jnp.concatenate([jnp.sin(ang_r), jnp.sin(ang_c)], axis=1)
    reps = LANES // (4 * ROPE_PAIRS)
    return jnp.tile(cos, (1, 2 * reps)), jnp.concatenate([jnp.tile(-sin, (1, reps)), jnp.tile(sin, (1, reps))], axis=1)


def _qkv(h, mod, g, w_perm, q_g, k_g, with_rope):
    b, s, d = h.shape
    nq = d
    nk = KV_HEADS * LANES
    t = 2048 if s % 2048 == 0 else _wide_row_tile(s)
    width = 2 * LANES
    lane = np.arange(width)
    head = 2 * (lane // LANES) + (lane % HEAD_DIM) // (HEAD_DIM // 2)
    same = (head[:, None] == head[None, :]).astype(np.float32)
    assert (nq + nk) % width == 0
    q_gain = _to_block_lanes(jnp.tile(q_g * (HEAD_DIM ** -0.5 * LOG2E), 2), 2)
    hg = jnp.concatenate([jnp.tile(q_gain, nq // LANES), jnp.tile(_to_block_lanes(k_g, 1), nk // LANES)])
    hg = hg.reshape(1, nq + nk)
    cos, sin = _rope_tables(s, with_rope)
    return pl.pallas_call(
        functools.partial(_qkv_kernel, nq=nq, nk=nk, sub=min(t, 256)),
        out_shape=(jax.ShapeDtypeStruct((b, s, nq), BF16),
                   jax.ShapeDtypeStruct((b, s, nk), BF16),
                   jax.ShapeDtypeStruct((b, s, nk), BF16)),
        grid=(b, s // t),
        in_specs=[pl.BlockSpec((1, t, d), lambda i, j: (i, j, 0)),
                  pl.BlockSpec((1, 6, d), lambda i, j: (i, 0, 0)),
                  _resident((1, d)), _resident(w_perm.shape),
                  _resident((width, width)), _resident((1, nq + nk)),
                  pl.BlockSpec((t, LANES), lambda i, j: (j, 0)),
                  pl.BlockSpec((t, LANES), lambda i, j: (j, 0))],
        out_specs=(pl.BlockSpec((1, t, nq), lambda i, j: (i, j, 0)),
                   pl.BlockSpec((1, t, nk), lambda i, j: (i, j, 0)),
                   pl.BlockSpec((1, t, nk), lambda i, j: (i, j, 0))),
        compiler_params=_params(2),
        name="qkv",
    )(h, mod, g.reshape(1, d), w_perm, jnp.asarray(same, BF16), hg, cos, sin)


def _attn_kernel(sink_ref, q_ref, *refs, band, s_len, n_ctx):
    if band:
        k_ref, v_ref, kc_ref, vc_ref, o_ref = refs
    else:
        kc_ref, vc_ref, o_ref = refs
    nq = ATTN_BLOCK
    n_blocks = q_ref.shape[1] // nq
    n_band = 3 * ATTN_BLOCK if band else 0
    n_keys = n_band + n_ctx
    group = q_ref.shape[2] // (KV_HEADS * HEAD_DIM)
    pairs = group // 2
    lane = lax.broadcasted_iota(jnp.int32, (1, LANES), 1)
    low = lane < HEAD_DIM
    first = (lane % HEAD_DIM) < HEAD_DIM // 2

    bias_band, k_bd, v_bd = [], [], []
    for qb in range(n_blocks):
        j = pl.program_id(1) * n_blocks + qb
        if band:
            start = pl.multiple_of(jnp.clip((j - 1) * ATTN_BLOCK, 0, s_len - n_band), ATTN_BLOCK)
            qpos = j * ATTN_BLOCK + lax.broadcasted_iota(jnp.int32, (nq, n_band), 0)
            kpos = start + lax.broadcasted_iota(jnp.int32, (nq, n_band), 1)
            bias_band.append(jnp.where(jnp.abs(qpos - kpos) <= WINDOW, 0.0, NEG_BIG).astype(F32))
        for kv in range(KV_HEADS):
            lanes_kv = slice(kv * LANES, (kv + 1) * LANES)
            k_all = kc_ref[0, :, lanes_kv]
            v_all = vc_ref[0, :, lanes_kv]
            if band:
                k_all = jnp.concatenate([k_ref[0, pl.ds(start, n_band), lanes_kv], k_all], axis=0)
                v_all = jnp.concatenate([v_ref[0, pl.ds(start, n_band), lanes_kv], v_all], axis=0)
            zero = jnp.zeros_like(k_all)
            k_bd.append(jnp.concatenate([jnp.where(first, k_all, zero), jnp.where(first, zero, k_all)], axis=0))
            v_bd.append(jnp.concatenate([jnp.where(low, v_all, zero), jnp.where(low, zero, v_all)], axis=0))

    def logits(qb, kv, m):
        blk = slice((kv * pairs + m) * LANES, (kv * pairs + m + 1) * LANES)
        return _dot_nt(q_ref[0, qb * nq:(qb + 1) * nq, blk], k_bd[qb * KV_HEADS + kv])

    def softmax(qb, kv, m, sc):
        probs, denoms = [], []
        for half in range(2):
            sh = sc[:, half * n_keys:(half + 1) * n_keys]
            if band:
                sh = jnp.concatenate([sh[:, :n_band] + bias_band[qb], sh[:, n_band:]], axis=1)
            sink = sink_ref[kv * group + 2 * m + half] * LOG2E
            mx = jnp.maximum(jnp.max(sh, axis=1, keepdims=True), sink)
            e = jnp.exp2(sh - mx)
            denoms.append(jnp.sum(e, axis=1, keepdims=True) + jnp.exp2(sink - mx))
            probs.append(e.astype(BF16))
        return jnp.concatenate(probs, axis=1), jnp.where(low, denoms[0], denoms[1])

    def weighted_values(qb, kv, m, p, denom):
        blk = slice((kv * pairs + m) * LANES, (kv * pairs + m + 1) * LANES)
        o_ref[0, qb * nq:(qb + 1) * nq, blk] = (_dot(p, v_bd[qb * KV_HEADS + kv]) / denom).astype(BF16)

    items = [(qb, kv, m) for qb in range(n_blocks) for kv in range(KV_HEADS) for m in range(pairs)]
    sc = logits(*items[0])
    pending = None
    for idx, item in enumerate(items):
        nxt = logits(*items[idx + 1]) if idx + 1 < len(items) else None
        p, denom = softmax(*item, sc)
        if pending is not None:
            weighted_values(*pending)
        pending = (*item, p, denom)
        sc = nxt
    weighted_values(*pending)


def _attention(q, k, v, kc, vc, sink, band):
    b, s, d = q.shape
    n_ctx = kc.shape[1]
    nkv = kc.shape[2]
    tq = next(n * ATTN_BLOCK for n in (8, 4, 2, 1) if s % (n * ATTN_BLOCK) == 0)
    in_specs = [pl.BlockSpec(memory_space=pltpu.SMEM),
                pl.BlockSpec((1, tq, d), lambda i, j: (i, j, 0))]
    args = [sink.astype(F32), q]
    if band:
        in_specs += [pl.BlockSpec((1, s, nkv), lambda i, j: (i, 0, 0))] * 2
        args += [k, v]
    in_specs += [pl.BlockSpec((1, n_ctx, nkv), lambda i, j: (i, 0, 0))] * 2
    args += [kc, vc]
    return pl.pallas_call(
        functools.partial(_attn_kernel, band=band, s_len=s, n_ctx=n_ctx),
        out_shape=jax.ShapeDtypeStruct((b, s, d), BF16),
        grid=(b, s // tq),
        in_specs=in_specs,
        out_specs=pl.BlockSpec((1, tq, d), lambda i, j: (i, j, 0)),
        compiler_params=_params(2),
        name="attn_band" if band else "attn_ctx",
    )(*args)


def _permute_qkv_columns(w_qkv, d):
    w = w_qkv.astype(BF16)
    nkv = KV_HEADS * HEAD_DIM
    wv = w[:, d + nkv:].reshape(-1, KV_HEADS, 1, HEAD_DIM)
    wv = jnp.broadcast_to(wv, (w.shape[0], KV_HEADS, LANES // HEAD_DIM, HEAD_DIM)).reshape(w.shape[0], -1)
    return jnp.concatenate([_to_block_lanes(w[:, :d], 2), _to_block_lanes(w[:, d:d + nkv], 1), wv], axis=1)


def _conv_kernel(h_ref, hp_ref, hn_ref, mod_ref, g_ref, w_in_ref, b_in_ref, dw_ref, dwb_ref, lng_ref,
                       lnb_ref, w_ref, b_ref, o_ref, n_ref, ext_ref, y_ref, *, rows):
    j = pl.program_id(1)
    t, d = h_ref.shape[1], h_ref.shape[2]
    n_chunks = d // LANES
    halo = CONV_HALO
    n_ref[0:halo, :] = _norm_mod(hp_ref[0], g_ref[...], mod_ref, 0).astype(BF16)
    n_ref[halo:halo + t, :] = _norm_mod(h_ref[0], g_ref[...], mod_ref, 0).astype(BF16)
    n_ref[halo + t:, :] = _norm_mod(hn_ref[0], g_ref[...], mod_ref, 0).astype(BF16)
    shift = halo - CONV_WIDTH // 2

    def glu(c):
        a = _dot(n_ref[...], w_in_ref[:, 2 * c * LANES:2 * (c + 1) * LANES]) + b_in_ref[:, 2 * c * LANES:2 * (c + 1) * LANES]
        u = a[:, :LANES] * jax.nn.sigmoid(a[:, LANES:])
        ext_ref[c, 0:halo, :] = jnp.where(j > 0, u[0:halo], 0.0)
        ext_ref[c, halo:halo + t, :] = u[halo:halo + t]
        ext_ref[c, halo + t:, :] = jnp.where(j < pl.num_programs(1) - 1, u[halo + t:], 0.0)

    def conv(c):
        for r0 in range(0, t, rows):
            acc = jnp.zeros((rows, LANES), F32)
            for tap in range(CONV_WIDTH):
                acc = acc + dw_ref[c, tap:tap + 1, :] * ext_ref[c, r0 + tap + shift:r0 + tap + shift + rows, :]
            y_ref[c, r0:r0 + rows, :] = acc

    glu(0)
    for c in range(n_chunks):
        if c + 1 < n_chunks:
            glu(c + 1)
        conv(c)
    y = jnp.concatenate([y_ref[c] for c in range(n_chunks)], axis=1)
    y = _layer_norm(y + dwb_ref[...], lng_ref[...], lnb_ref[...])
    y = (y * jax.nn.sigmoid(y)).astype(BF16)
    acc = _dot(y, w_ref[...]) + b_ref[...]
    o_ref[0] = h_ref[0] + mod_ref[0, 2:3, :] * acc


def _conv_module(h, mod, g, w_in, b_in, dw, dw_b, ln_g, ln_b, w_out, b_out):
    b, s, d = h.shape
    t = _wide_row_tile(s)
    hb = t // CONV_HALO
    last = s // CONV_HALO - 1
    n_chunks = d // LANES
    taps_pad = -(-CONV_WIDTH // SUBLANES) * SUBLANES
    dw_pad = jnp.concatenate([dw, jnp.zeros((taps_pad - CONV_WIDTH, d), F32)], axis=0)
    dw_pad = dw_pad.reshape(taps_pad, n_chunks, LANES).transpose(1, 0, 2)

    def chunk_major(x):
        lead = x.shape[:-1]
        return x.reshape(*lead, 2, n_chunks, LANES).swapaxes(-2, -3).reshape(*lead, 2 * d)

    return pl.pallas_call(
        functools.partial(_conv_kernel, rows=64),
        out_shape=jax.ShapeDtypeStruct(h.shape, F32),
        grid=(b, s // t),
        in_specs=[pl.BlockSpec((1, t, d), lambda i, j: (i, j, 0)),
                  pl.BlockSpec((1, CONV_HALO, d), lambda i, j: (i, jnp.maximum(j * hb - 1, 0), 0)),
                  pl.BlockSpec((1, CONV_HALO, d), lambda i, j: (i, jnp.minimum((j + 1) * hb, last), 0)),
                  pl.BlockSpec((1, 6, d), lambda i, j: (i, 0, 0)),
                  _resident((1, d)), _resident((d, 2 * d)), _resident((1, 2 * d)),
                  _resident((n_chunks, taps_pad, LANES)), _resident((1, d)), _resident((1, d)), _resident((1, d)),
                  _resident((d, d)), _resident((1, d))],
        out_specs=pl.BlockSpec((1, t, d), lambda i, j: (i, j, 0)),
        scratch_shapes=[pltpu.VMEM((t + 2 * CONV_HALO, d), BF16),
                        pltpu.VMEM((n_chunks, t + 2 * CONV_HALO, LANES), F32),
                        pltpu.VMEM((n_chunks, t, LANES), F32)],
        compiler_params=_params(2),
        name="conv",
    )(h, h, h, mod, g.reshape(1, d), chunk_major(w_in.astype(BF16)), chunk_major(b_in.reshape(1, 2 * d)),
      dw_pad, dw_b.reshape(1, d), ln_g.reshape(1, d), ln_b.reshape(1, d), w_out.astype(BF16), b_out.reshape(1, d))


def _gmlp_kernel(h_ref, mod_ref, g_ref, w_in_ref, b_in_ref, lng_ref, lnb_ref, ws_ref, bs_ref, w_out_ref,
                 o_ref, gate_ref, *, sub):
    t, d = h_ref.shape[1], h_ref.shape[2]
    gd = d // GMLP_GROUPS

    def project(r0):
        n = _norm_mod(h_ref[0, r0:r0 + sub, :], g_ref[...], mod_ref, 0).astype(BF16)
        return _dot(n, w_in_ref[...])

    def gate(r0, acc):
        z = jax.nn.gelu(acc + b_in_ref[...])
        u = z[:, :d]
        v = _layer_norm(z[:, d:], lng_ref[...], lnb_ref[...]).astype(BF16)
        for c in range(sub // GMLP_CHUNK):
            rs = slice(c * GMLP_CHUNK, (c + 1) * GMLP_CHUNK)
            for gi in range(GMLP_GROUPS):
                ls = slice(gi * gd, (gi + 1) * gd)
                sg = _dot(ws_ref[gi], v[rs, ls]) + jnp.tile(bs_ref[gi], (1, gd // LANES))
                gate_ref[r0 + c * GMLP_CHUNK:r0 + (c + 1) * GMLP_CHUNK, ls] = (u[rs, ls] * sg).astype(BF16)

    def project_out(r0):
        y = _dot(gate_ref[r0:r0 + sub, :], w_out_ref[...])
        o_ref[0, r0:r0 + sub, :] = h_ref[0, r0:r0 + sub, :] + mod_ref[0, 2:3, :] * y

    acc = project(0)
    for r0 in range(0, t, sub):
        nxt = project(r0 + sub) if r0 + sub < t else None
        gate(r0, acc)
        if r0 > 0:
            project_out(r0 - sub)
        acc = nxt
    project_out(t - sub)


def _chunk_gmlp(h, mod, g, w_in, b_in, ln_g, ln_b, w_s, b_s, w_out):
    b, s, d = h.shape
    t = 1024 if s % 1024 == 0 else (256 if s % 256 == 0 else GMLP_CHUNK)
    bs = jnp.broadcast_to(b_s[:, :, None], (GMLP_GROUPS, GMLP_CHUNK, LANES)).astype(F32)
    return pl.pallas_call(
        functools.partial(_gmlp_kernel, sub=min(t, 256)),
        out_shape=jax.ShapeDtypeStruct(h.shape, F32),
        grid=(b, s // t),
        in_specs=[pl.BlockSpec((1, t, d), lambda i, j: (i, j, 0)),
                  pl.BlockSpec((1, 6, d), lambda i, j: (i, 0, 0)),
                  _resident((1, d)), _resident((d, 2 * d)), _resident((1, 2 * d)),
                  _resident((1, d)), _resident((1, d)),
                  _resident((GMLP_GROUPS, GMLP_CHUNK, GMLP_CHUNK)), _resident((GMLP_GROUPS, GMLP_CHUNK, LANES)),
                  _resident((d, d))],
        out_specs=pl.BlockSpec((1, t, d), lambda i, j: (i, j, 0)),
        scratch_shapes=[pltpu.VMEM((t, d), BF16)],
        compiler_params=_params(2),
        name="gmlp",
    )(h, mod, g.reshape(1, d), w_in.astype(BF16), b_in.reshape(1, 2 * d), ln_g.reshape(1, d),
      ln_b.reshape(1, d), w_s.astype(BF16), bs, w_out.astype(BF16))


def kernel(x, c, ctx, c_ctx, ada_w, ada_b, norm1_g, norm2_g, mlp_w1, mlp_w2, fnet_w, fnet_b, attn_w_qkv, attn_q_g, attn_k_g, attn_sink, attn_w_o, conv_w_in, conv_b_in, conv_dw, conv_dw_b, conv_ln_g, conv_ln_b, conv_w_out, conv_b_out, gmlp_w_in, gmlp_b_in, gmlp_ln_g, gmlp_ln_b, gmlp_w_s, gmlp_b_s, gmlp_w_out):
    bsz, _, d = x.shape
    depth = ada_w.shape[0]
    n_rows = -(-(bsz + 1) // 8) * 8
    c_rows = jnp.concatenate([c, c_ctx[None, :], jnp.zeros((n_rows - bsz - 1, d), F32)], axis=0)
    mods = _adaln(c_rows, ada_w, ada_b).reshape(depth, n_rows, 6, d)

    w1_all = mlp_w1.astype(BF16)
    w2_all = mlp_w2.astype(BF16)
    h_lat, h_ctx = x, ctx
    for i in range(depth):
        last = i == depth - 1
        kind, li = i % N_MIXERS, i // N_MIXERS
        mod_lat = mods[i, :bsz]
        mod_ctx = jnp.broadcast_to(mods[i, bsz], (bsz, 6, d))
        streams = [(h_lat, mod_lat)] + ([] if last else [(h_ctx, mod_ctx)])
        mixed = []
        if kind == 1:
            w_perm = _permute_qkv_columns(attn_w_qkv[li], d)
            w_o = attn_w_o[li].astype(BF16)
            zero_b = jnp.zeros((d,), F32)
            q, k, v = _qkv(h_lat, mod_lat, norm1_g[i], w_perm, attn_q_g[li], attn_k_g[li], True)
            qc, kc, vc = _qkv(h_ctx, mod_ctx, norm1_g[i], w_perm, attn_q_g[li], attn_k_g[li], False)
            mixed.append((h_lat, (_attention(q, k, v, kc, vc, attn_sink[li], True), w_o, zero_b)))
            if not last:
                mixed.append((h_ctx, (_attention(qc, None, None, kc, vc, attn_sink[li], False), w_o, zero_b)))
        else:
            for h, mod in streams:
                if kind == 0:
                    mixed.append((h, (_fourier_mixer(h, mod, norm1_g[i]), fnet_w[li].astype(BF16), fnet_b[li])))
                elif kind == 2:
                    mixed.append((_conv_module(h, mod, norm1_g[i], conv_w_in[li], conv_b_in[li], conv_dw[li],
                                               conv_dw_b[li], conv_ln_g[li], conv_ln_b[li], conv_w_out[li],
                                               conv_b_out[li]), None))
                else:
                    mixed.append((_chunk_gmlp(h, mod, norm1_g[i], gmlp_w_in[li], gmlp_b_in[li], gmlp_ln_g[li],
                                              gmlp_ln_b[li], gmlp_w_s[li], gmlp_b_s[li], gmlp_w_out[li]), None))
        h_lat = _mlp(mixed[0][0], mod_lat, norm2_g[i], w1_all, w2_all, i, mixed[0][1])
        if not last:
            hc, pend = mixed[1]
            if pend is not None:
                pend = (pend[0].reshape(1, -1, pend[0].shape[-1]), pend[1], pend[2])
            h_ctx = _mlp(hc.reshape(1, -1, d), mod_ctx[:1], norm2_g[i], w1_all, w2_all, i, pend).reshape(hc.shape)
    return h_lat
```
